```python
import math
import jax, jax.numpy as jnp
from jax import lax
import numpy as np

D_MODEL = 1024
BATCH = 2
SEQ = 8192
DEPTH = 2

CHUNK = 64
CONV_WIDTH = 512
CONV_GROUPS = 8
CONV_TAPS = 3
N_HEADS = 8
HEAD_DIM = 64
ATTN_WIDTH = N_HEADS * HEAD_DIM
D_FF = 4 * D_MODEL
Q_BLOCK = 128
N_MOD = 6
EPS = 1e-6
IN_COLS = 3 * CONV_WIDTH + 3 * ATTN_WIDTH + 2 * D_MODEL

kernel_name = "hybrid_shortconv_stickbreaking_block"


def rms_norm(x, g):
    xf = x.astype(jnp.float32)
    y = xf * lax.rsqrt(jnp.mean(xf * xf, axis=-1, keepdims=True) + EPS)
    return (y * g.astype(jnp.float32)).astype(x.dtype)


def modulate(h, shift, scale):
    return h * (1.0 + scale[:, None, :]) + shift[:, None, :]


def short_conv_branch(b_gate, c_gate, u, conv_w):
    s = u.shape[1]
    v = c_gate * u
    vp = jnp.pad(v, ((0, 0), (CONV_TAPS - 1, 0), (0, 0)))
    y = sum(conv_w[k] * vp[:, k:k + s, :] for k in range(CONV_TAPS))
    return b_gate * y


def stick_breaking_attention(q, k, v):
    b, s, h, dh = q.shape
    qh = jnp.transpose(q, (0, 2, 1, 3))
    kh = jnp.transpose(k, (0, 2, 1, 3))
    vh = jnp.transpose(v, (0, 2, 1, 3))
    inv_sqrt = 1.0 / math.sqrt(dh)
    key_pos = jnp.arange(s)
    n_blocks = s // Q_BLOCK

    def block(i):
        start = i * Q_BLOCK
        q_blk = lax.dynamic_slice_in_dim(qh, start, Q_BLOCK, axis=2)
        z = jnp.einsum('bhqd,bhkd->bhqk', q_blk, kh).astype(jnp.float32) * inv_sqrt
        q_pos = start + jnp.arange(Q_BLOCK)
        mask = key_pos[None, :] < q_pos[:, None]
        log_1m_beta = jnp.where(mask, jax.nn.log_sigmoid(-z), 0.0)
        suffix = lax.cumsum(log_1m_beta, axis=3, reverse=True) - log_1m_beta
        a = jnp.where(mask, jnp.exp(jax.nn.log_sigmoid(z) + suffix), 0.0)
        return jnp.einsum('bhqk,bhkd->bhqd', a, vh.astype(jnp.float32)).astype(q.dtype)

    o = lax.map(block, jnp.arange(n_blocks))
    o = jnp.transpose(o, (1, 0, 3, 2, 4))
    return o.reshape(b, s, h * dh)


def setup_inputs(seed: int = 0) -> dict:
    key = jax.random.key(seed)
    ks = jax.random.split(key, 16)
    f32 = jnp.float32
    L, D = DEPTH, D_MODEL

    def nrm(k, shape, fan_in):
        return jax.random.normal(k, shape, f32) * (fan_in ** -0.5)

    def gain(k):
        return 1.0 + 0.05 * jax.random.normal(k, (L, D), f32)

    return {
        "x": jax.random.normal(ks[0], (BATCH, SEQ, D), f32),
        "c": jax.random.normal(ks[1], (BATCH, D), f32),
        "w_ada": nrm(ks[2], (L, D, N_MOD * D), D) * 0.5,
        "b_ada": 0.01 * jax.random.normal(ks[3], (L, N_MOD * D), f32),
        "g_pre_mix": gain(ks[4]),
        "g_post_mix": gain(ks[5]),
        "g_pre_mlp": gain(ks[6]),
        "g_post_mlp": gain(ks[7]),
        "w_in": nrm(ks[8], (L, D, IN_COLS), D),
        "conv_w": nrm(ks[9], (L, CONV_TAPS, CONV_WIDTH), CONV_TAPS),
        "w_proj_conv": nrm(ks[10], (L, CONV_WIDTH, D), CONV_WIDTH),
        "w_proj_attn": nrm(ks[11], (L, ATTN_WIDTH, D), ATTN_WIDTH),
        "w_out": nrm(ks[12], (L, D, D), D),
        "w_mlp_in": nrm(ks[13], (L, D, D_FF), D),
        "w_mlp_out": nrm(ks[14], (L, D_FF, D), D_FF),
    }


def reference(x, c, w_ada, b_ada, g_pre_mix, g_post_mix, g_pre_mlp, g_post_mlp,
              w_in, conv_w, w_proj_conv, w_proj_attn, w_out, w_mlp_in, w_mlp_out):
    b, s, d = x.shape
    split_pts = np.cumsum([CONV_WIDTH, CONV_WIDTH, CONV_WIDTH,
                           ATTN_WIDTH, ATTN_WIDTH, ATTN_WIDTH, D_MODEL])
    for l in range(DEPTH):
        mod = c @ w_ada[l] + b_ada[l]
        sh1, sc1, gt1, sh2, sc2, gt2 = jnp.split(mod, N_MOD, axis=-1)

        h = modulate(rms_norm(x, g_pre_mix[l]), sh1, sc1)
        proj = h @ w_in[l]
        bg, cg, u, q, k, v, ga, gb = jnp.split(proj, split_pts, axis=-1)
        y_conv = short_conv_branch(bg, cg, u, conv_w[l]) @ w_proj_conv[l]
        o = stick_breaking_attention(q.reshape(b, s, N_HEADS, HEAD_DIM),
                                     k.reshape(b, s, N_HEADS, HEAD_DIM),
                                     v.reshape(b, s, N_HEADS, HEAD_DIM))
        y_attn = o @ w_proj_attn[l]
        merged = jax.nn.sigmoid(ga) * y_conv + jax.nn.sigmoid(gb) * y_attn
        mix_out = merged @ w_out[l]
        x = x + gt1[:, None, :] * rms_norm(mix_out, g_post_mix[l])

        h2 = modulate(rms_norm(x, g_pre_mlp[l]), sh2, sc2)
        ff = jnp.square(jax.nn.relu(h2 @ w_mlp_in[l])) @ w_mlp_out[l]
        x = x + gt2[:, None, :] * rms_norm(ff, g_post_mlp[l])
    return x
```

```python
import functools
import math

import jax
import jax.numpy as jnp
from jax import lax
from jax.experimental import pallas as pl
from jax.experimental.pallas import tpu as pltpu

CONV_WIDTH = 512
CONV_TAPS = 3
ATTN_WIDTH = 512
HEAD_DIM = 64
N_MOD = 6
EPS = 1e-6

LANES = 128
HEADS_PER_TILE = LANES // HEAD_DIM
SEQ_TILE = 512
Q_TILE = 256
K_TILE = 256
FF_CHUNK = 1024
VMEM_LIMIT = 56 * 1024 * 1024
LOG2E = math.log2(math.e)

_f32 = jnp.float32
_bf16 = jnp.bfloat16


def _dot(a, b):
    return jnp.dot(a, b, preferred_element_type=_f32)


def _rms(x, gain):
    return x * lax.rsqrt(jnp.mean(x * x, axis=-1, keepdims=True) + EPS) * gain


def _const_spec(shape):
    return pl.BlockSpec(shape, lambda *_: (0,) * len(shape), pipeline_mode=pl.Buffered(1))


def _ada_kernel(c_ref, w_ref, b_ref, o_ref):
    o_ref[0] = jnp.dot(c_ref[...], w_ref[0], preferred_element_type=_f32,
                       precision=lax.Precision.HIGHEST) + b_ref[0, 0]


def _ada(c, w_ada, b_ada):
    depth, d, _ = w_ada.shape
    b = c.shape[0]
    return pl.pallas_call(
        _ada_kernel,
        grid=(depth, N_MOD),
        in_specs=[
            pl.BlockSpec((b, d), lambda l, j: (0, 0)),
            pl.BlockSpec((1, d, d), lambda l, j: (l, 0, j)),
            pl.BlockSpec((1, 1, 1, d), lambda l, j: (l, j, 0, 0)),
        ],
        out_specs=pl.BlockSpec((1, b, d), lambda l, j: (l, 0, j)),
        out_shape=jax.ShapeDtypeStruct((depth, b, N_MOD * d), _f32),
        compiler_params=pltpu.CompilerParams(
            dimension_semantics=("arbitrary", "arbitrary"), vmem_limit_bytes=VMEM_LIMIT),
    )(c, w_ada, b_ada.reshape(depth, N_MOD, 1, d))


def _mixer_in_kernel(x_ref, mod_ref, g_ref, w_in_ref, cw_ref, wpc_ref,
                     q_ref, k_ref, v_ref, gc_ref, sgb_ref, carry_ref):
    d = x_ref.shape[-1]
    tm = x_ref.shape[1]
    cw = CONV_WIDTH

    @pl.when(pl.program_id(1) == 0)
    def _():
        carry_ref[...] = jnp.zeros_like(carry_ref)

    shift = mod_ref[0, :, 0:d]
    scale = mod_ref[0, :, d:2 * d]
    x = x_ref[0]
    h = (_rms(x, g_ref[...]) * (1.0 + scale) + shift).astype(_bf16)

    bg = _dot(h, w_in_ref[:, 0:cw])
    cg = _dot(h, w_in_ref[:, cw:2 * cw])
    u = _dot(h, w_in_ref[:, 2 * cw:3 * cw])
    vv = cg * u
    row = lax.broadcasted_iota(jnp.int32, vv.shape, 0)
    prev1 = carry_ref[1:2, :]
    prev2 = carry_ref[0:1, :]
    v1 = jnp.where(row == 0, prev1, pltpu.roll(vv, 1, axis=0))
    v2 = jnp.where(row == 0, prev2, jnp.where(row == 1, prev1, pltpu.roll(vv, 2, axis=0)))
    carry_ref[...] = vv[tm - 2:tm, :]
    conv = cw_ref[0:1, :] * v2 + cw_ref[1:2, :] * v1 + cw_ref[2:3, :] * vv
    y_conv = _dot((bg * conv).astype(_bf16), wpc_ref[...])

    o = 3 * cw
    aw = ATTN_WIDTH
    q_ref[0] = (_dot(h, w_in_ref[:, o:o + aw]) * (LOG2E / math.sqrt(HEAD_DIM))).astype(_bf16)
    k_ref[0] = _dot(h, w_in_ref[:, o + aw:o + 2 * aw]).astype(_bf16)
    v_ref[0] = _dot(h, w_in_ref[:, o + 2 * aw:o + 3 * aw]).astype(_bf16)
    o += 3 * aw
    ga = _dot(h, w_in_ref[:, o:o + d])
    gc_ref[0] = jax.nn.sigmoid(ga) * y_conv
    gb = _dot(h, w_in_ref[:, o + d:o + 2 * d])
    sgb_ref[0] = jax.nn.sigmoid(gb)


def _mixer_in(x, mod, g_pre, w_in, conv_w, w_proj_conv):
    b, s, d = x.shape
    tm = SEQ_TILE
    tile = lambda width: pl.BlockSpec((1, tm, width), lambda i, j: (i, j, 0))
    return pl.pallas_call(
        _mixer_in_kernel,
        grid=(b, s // tm),
        in_specs=[
            tile(d),
            pl.BlockSpec((1, 1, N_MOD * d), lambda i, j: (i, 0, 0)),
            _const_spec((1, d)),
            _const_spec(w_in.shape),
            _const_spec(conv_w.shape),
            _const_spec(w_proj_conv.shape),
        ],
        out_specs=[tile(ATTN_WIDTH), tile(ATTN_WIDTH), tile(ATTN_WIDTH), tile(d), tile(d)],
        out_shape=[
            jax.ShapeDtypeStruct((b, s, ATTN_WIDTH), _bf16),
            jax.ShapeDtypeStruct((b, s, ATTN_WIDTH), _bf16),
            jax.ShapeDtypeStruct((b, s, ATTN_WIDTH), _bf16),
            jax.ShapeDtypeStruct((b, s, d), _f32),
            jax.ShapeDtypeStruct((b, s, d), _f32),
        ],
        scratch_shapes=[pltpu.VMEM((CONV_TAPS - 1, CONV_WIDTH), _f32)],
        compiler_params=pltpu.CompilerParams(
            dimension_semantics=("arbitrary", "arbitrary"), vmem_limit_bytes=VMEM_LIMIT),
    )(x, mod.reshape(b, 1, N_MOD * d), g_pre.reshape(1, d), w_in, conv_w, w_proj_conv)


def _attn_kernel(q_ref, k_ref, v_ref, o_ref, q2_ref, acc_ref, run_ref):
    tq = q_ref.shape[1]
    qi = pl.program_id(2)

    lane = lax.broadcasted_iota(jnp.int32, (tq, LANES), 1)
    q = q_ref[0]
    zero = jnp.zeros_like(q)
    q2_ref[0:tq, :] = jnp.where(lane < HEAD_DIM, q, zero)
    q2_ref[tq:2 * tq, :] = jnp.where(lane >= HEAD_DIM, q, zero)
    acc_ref[...] = jnp.zeros_like(acc_ref)
    run_ref[...] = jnp.zeros_like(run_ref)

    tri = (lax.broadcasted_iota(jnp.int32, (K_TILE, K_TILE), 0)
           >= lax.broadcasted_iota(jnp.int32, (K_TILE, K_TILE), 1)).astype(_bf16)

    def key_block(kb, masked):
        start = pl.multiple_of(kb * K_TILE, K_TILE)
        kblk = k_ref[0, pl.ds(start, K_TILE), :]
        vblk = v_ref[0, pl.ds(start, K_TILE), :]
        z = lax.dot_general(q2_ref[...], kblk, (((1,), (1,)), ((), ())),
                            preferred_element_type=_f32)
        l = jnp.minimum(-z, 0.0) - jnp.log2(1.0 + jnp.exp2(-jnp.abs(z)))
        if masked:
            row = lax.broadcasted_iota(jnp.int32, z.shape, 0) & (tq - 1)
            col = lax.broadcasted_iota(jnp.int32, z.shape, 1)
            keep = col < row
            l = jnp.where(keep, l, 0.0)
        hi = l.astype(_bf16)
        lo = (l - hi.astype(_f32)).astype(_bf16)
        suffix = _dot(hi, tri) + _dot(lo, tri)
        a = jnp.exp2(z + suffix + run_ref[...])
        if masked:
            a = jnp.where(keep, a, 0.0)
        acc_ref[...] += _dot(a.astype(_bf16), vblk)
        run_ref[...] += suffix[:, 0:1]

    key_block(qi, True)

    def body(i, carry):
        key_block(qi - 1 - i, False)
        return carry

    lax.fori_loop(0, qi, body, 0)

    acc = acc_ref[...]
    o_ref[0] = jnp.where(lane < HEAD_DIM, acc[0:tq], acc[tq:2 * tq]).astype(o_ref.dtype)


def _attention(q, k, v):
    b, s, w = q.shape
    assert Q_TILE == K_TILE and s % Q_TILE == 0 and w % LANES == 0
    rows = HEADS_PER_TILE * Q_TILE
    seq = pl.BlockSpec((1, s, LANES), lambda i, h, j: (i, 0, h))
    tile = pl.BlockSpec((1, Q_TILE, LANES), lambda i, h, j: (i, j, h))
    return pl.pallas_call(
        _attn_kernel,
        grid=(b, w // LANES, s // Q_TILE),
        in_specs=[tile, seq, seq],
        out_specs=tile,
        out_shape=jax.ShapeDtypeStruct((b, s, w), _bf16),
        scratch_shapes=[
            pltpu.VMEM((rows, LANES), _bf16),
            pltpu.VMEM((rows, LANES), _f32),
            pltpu.VMEM((rows, 1), _f32),
        ],
        compiler_params=pltpu.CompilerParams(
            dimension_semantics=("arbitrary", "arbitrary", "arbitrary"),
            vmem_limit_bytes=VMEM_LIMIT),
    )(q, k, v)


def _mixer_out_kernel(x_ref, o_ref, gc_ref, sgb_ref, mod_ref, gains_ref,
                      wpa_ref, wout_ref, w1_ref, w2_ref, out_ref):
    d = x_ref.shape[-1]
    mod = lambda i: mod_ref[0, :, i * d:(i + 1) * d]
    gate1, shift2, scale2, gate2 = mod(2), mod(3), mod(4), mod(5)
    g_post_mix = gains_ref[0:1, :]
    g_pre_mlp = gains_ref[1:2, :]
    g_post_mlp = gains_ref[2:3, :]

    y_attn = _dot(o_ref[0], wpa_ref[...])
    merged = gc_ref[0] + sgb_ref[0] * y_attn
    mix = _dot(merged.astype(_bf16), wout_ref[...])
    x1 = x_ref[0] + gate1 * _rms(mix, g_post_mix)

    h2 = (_rms(x1, g_pre_mlp) * (1.0 + scale2) + shift2).astype(_bf16)
    ff = jnp.zeros_like(x1)
    for j in range(0, w1_ref.shape[1], FF_CHUNK):
        hid = jnp.maximum(_dot(h2, w1_ref[:, j:j + FF_CHUNK]), 0.0)
        ff = ff + _dot((hid * hid).astype(_bf16), w2_ref[j:j + FF_CHUNK, :])
    out_ref[0] = x1 + gate2 * _rms(ff, g_post_mlp)


def _mixer_out(x, o, gc, sgb, mod, gains, w_proj_attn, w_out, w1, w2):
    b, s, d = x.shape
    tm = SEQ_TILE
    tile = lambda width: pl.BlockSpec((1, tm, width), lambda i, j: (i, j, 0))
    return pl.pallas_call(
        _mixer_out_kernel,
        grid=(b, s // tm),
        in_specs=[
            tile(d), tile(ATTN_WIDTH), tile(d), tile(d),
            pl.BlockSpec((1, 1, N_MOD * d), lambda i, j: (i, 0, 0)),
            _const_spec(gains.shape),
            _const_spec(w_proj_attn.shape),
            _const_spec(w_out.shape),
            _const_spec(w1.shape),
            _const_spec(w2.shape),
        ],
        out_specs=tile(d),
        out_shape=jax.ShapeDtypeStruct((b, s, d), _f32),
        compiler_params=pltpu.CompilerParams(
            dimension_semantics=("arbitrary", "arbitrary"), vmem_limit_bytes=VMEM_LIMIT),
    )(x, o, gc, sgb, mod.reshape(b, 1, N_MOD * d), gains, w_proj_attn, w_out, w1, w2)


def kernel(x, c, w_ada, b_ada, g_pre_mix, g_post_mix, g_pre_mlp, g_post_mlp, w_in, conv_w,
           w_proj_conv, w_proj_attn, w_out, w_mlp_in, w_mlp_out):
    depth = w_ada.shape[0]
    mod = _ada(c, w_ada, b_ada)
    for l in range(depth):
        q, k, v, gc, sgb = _mixer_in(x, mod[l], g_pre_mix[l], w_in[l].astype(_bf16), conv_w[l],
                                     w_proj_conv[l].astype(_bf16))
        o = _attention(q, k, v)
        gains = jnp.stack([g_post_mix[l], g_pre_mlp[l], g_post_mlp[l]])
        x = _mixer_out(x, o, gc, sgb, mod[l], gains, w_proj_attn[l].astype(_bf16),
                       w_out[l].astype(_bf16), w_mlp_in[l].astype(_bf16),
                       w_mlp_out[l].astype(_bf16))
    return x
```

```python
import functools
import math

import jax
import jax.numpy as jnp
from jax import lax
from jax.experimental import pallas as pl
from jax.experimental.pallas import tpu as pltpu

CONV_WIDTH = 512
CONV_TAPS = 3
ATTN_WIDTH = 512
HEAD_DIM = 64
N_MOD = 6
EPS = 1e-6

LANES = 128
HEADS_PER_TILE = LANES // HEAD_DIM
SEQ_TILE = 512
Q_TILE = 256
K_TILE = 256
FF_CHUNK = 1024
VMEM_LIMIT = 56 * 1024 * 1024
LOG2E = math.log2(math.e)
UNDERFLOW_LOG2 = 160.0
Z_CLAMP = 120.0

_f32 = jnp.float32
_bf16 = jnp.bfloat16


def _dot(a, b):
    return jnp.dot(a, b, preferred_element_type=_f32)


def _rms(x, gain):
    return x * lax.rsqrt(jnp.mean(x * x, axis=-1, keepdims=True) + EPS) * gain


def _const_spec(shape):
    return pl.BlockSpec(shape, lambda *_: (0,) * len(shape), pipeline_mode=pl.Buffered(1))


def _ada_kernel(c_ref, w_ref, b_ref, o_ref):
    o_ref[0] = jnp.dot(c_ref[...], w_ref[0], preferred_element_type=_f32,
                       precision=lax.Precision.HIGHEST) + b_ref[0, 0]


def _ada(c, w_ada, b_ada):
    depth, d, _ = w_ada.shape
    b = c.shape[0]
    return pl.pallas_call(
        _ada_kernel,
        grid=(depth, N_MOD),
        in_specs=[
            pl.BlockSpec((b, d), lambda l, j: (0, 0)),
            pl.BlockSpec((1, d, d), lambda l, j: (l, 0, j)),
            pl.BlockSpec((1, 1, 1, d), lambda l, j: (l, j, 0, 0)),
        ],
        out_specs=pl.BlockSpec((1, b, d), lambda l, j: (l, 0, j)),
        out_shape=jax.ShapeDtypeStruct((depth, b, N_MOD * d), _f32),
        compiler_params=pltpu.CompilerParams(
            dimension_semantics=("arbitrary", "arbitrary"), vmem_limit_bytes=VMEM_LIMIT),
        name="ada",
    )(c, w_ada, b_ada.reshape(depth, N_MOD, 1, d))


def _mixer_in_kernel(x_ref, mod_ref, g_ref, w_in_ref, cw_ref, wpc_ref,
                     q_ref, k_ref, v_ref, gc_ref, sgb_ref, carry_ref):
    d = x_ref.shape[-1]
    tm = x_ref.shape[1]
    cw = CONV_WIDTH

    @pl.when(pl.program_id(1) == 0)
    def _():
        carry_ref[...] = jnp.zeros_like(carry_ref)

    shift = mod_ref[0, :, 0:d]
    scale = mod_ref[0, :, d:2 * d]
    x = x_ref[0]
    h = (_rms(x, g_ref[...]) * (1.0 + scale) + shift).astype(_bf16)

    bg = _dot(h, w_in_ref[:, 0:cw])
    cg = _dot(h, w_in_ref[:, cw:2 * cw])
    u = _dot(h, w_in_ref[:, 2 * cw:3 * cw])
    vv = cg * u
    row = lax.broadcasted_iota(jnp.int32, vv.shape, 0)
    prev1 = carry_ref[1:2, :]
    prev2 = carry_ref[0:1, :]
    v1 = jnp.where(row == 0, prev1, pltpu.roll(vv, 1, axis=0))
    v2 = jnp.where(row == 0, prev2, jnp.where(row == 1, prev1, pltpu.roll(vv, 2, axis=0)))
    carry_ref[...] = vv[tm - 2:tm, :]
    conv = cw_ref[0:1, :] * v2 + cw_ref[1:2, :] * v1 + cw_ref[2:3, :] * vv
    y_conv = _dot((bg * conv).astype(_bf16), wpc_ref[...])

    o = 3 * cw
    aw = ATTN_WIDTH
    q_ref[0] = (_dot(h, w_in_ref[:, o:o + aw]) * (LOG2E / math.sqrt(HEAD_DIM))).astype(_bf16)
    k_ref[0] = _dot(h, w_in_ref[:, o + aw:o + 2 * aw]).astype(_bf16)
    v_ref[0] = _dot(h, w_in_ref[:, o + 2 * aw:o + 3 * aw]).astype(_bf16)
    o += 3 * aw
    ga = _dot(h, w_in_ref[:, o:o + d])
    gc_ref[0] = jax.nn.sigmoid(ga) * y_conv
    gb = _dot(h, w_in_ref[:, o + d:o + 2 * d])
    sgb_ref[0] = jax.nn.sigmoid(gb)


def _mixer_in(x, mod, g_pre, w_in, conv_w, w_proj_conv):
    b, s, d = x.shape
    tm = SEQ_TILE
    tile = lambda width: pl.BlockSpec((1, tm, width), lambda i, j: (i, j, 0))
    return pl.pallas_call(
        _mixer_in_kernel,
        grid=(b, s // tm),
        in_specs=[
            tile(d),
            pl.BlockSpec((1, 1, N_MOD * d), lambda i, j: (i, 0, 0)),
            _const_spec((1, d)),
            _const_spec(w_in.shape),
            _const_spec(conv_w.shape),
            _const_spec(w_proj_conv.shape),
        ],
        out_specs=[tile(ATTN_WIDTH), tile(ATTN_WIDTH), tile(ATTN_WIDTH), tile(d), tile(d)],
        out_shape=[
            jax.ShapeDtypeStruct((b, s, ATTN_WIDTH), _bf16),
            jax.ShapeDtypeStruct((b, s, ATTN_WIDTH), _bf16),
            jax.ShapeDtypeStruct((b, s, ATTN_WIDTH), _bf16),
            jax.ShapeDtypeStruct((b, s, d), _f32),
            jax.ShapeDtypeStruct((b, s, d), _f32),
        ],
        scratch_shapes=[pltpu.VMEM((CONV_TAPS - 1, CONV_WIDTH), _f32)],
        compiler_params=pltpu.CompilerParams(
            dimension_semantics=("arbitrary", "arbitrary"), vmem_limit_bytes=VMEM_LIMIT),
        name="mixer_in",
    )(x, mod.reshape(b, 1, N_MOD * d), g_pre.reshape(1, d), w_in, conv_w, w_proj_conv)


def _attn_kernel(q_ref, k_ref, v_ref, o_ref, qh_ref, acc_ref, run_ref):
    tq = q_ref.shape[1]
    qi = pl.program_id(2)

    lane = lax.broadcasted_iota(jnp.int32, (tq, LANES), 1)
    q = q_ref[0]
    zero = jnp.zeros_like(q)
    qh_ref[0] = jnp.where(lane < HEAD_DIM, q, zero)
    qh_ref[1] = jnp.where(lane >= HEAD_DIM, q, zero)
    acc_ref[...] = jnp.zeros_like(acc_ref)
    run_ref[...] = jnp.zeros_like(run_ref)

    tri2 = ((lax.broadcasted_iota(jnp.int32, (2 * K_TILE, K_TILE), 0) & (K_TILE - 1))
            >= lax.broadcasted_iota(jnp.int32, (2 * K_TILE, K_TILE), 1)).astype(_bf16)

    def key_block(kb, masked):
        start = pl.multiple_of(kb * K_TILE, K_TILE)
        kblk = k_ref[0, pl.ds(start, K_TILE), :]
        vblk = v_ref[0, pl.ds(start, K_TILE), :]
        if masked:
            keep = (lax.broadcasted_iota(jnp.int32, (tq, K_TILE), 1)
                    < lax.broadcasted_iota(jnp.int32, (tq, K_TILE), 0))
        heads = range(HEADS_PER_TILE)
        z = [lax.dot_general(qh_ref[h], kblk, (((1,), (1,)), ((), ())),
                             preferred_element_type=_f32) for h in heads]
        suffix = []
        for h in heads:
            sp = jnp.maximum(jnp.log2(1.0 + jnp.exp2(jnp.minimum(z[h], Z_CLAMP))), z[h])
            if masked:
                sp = jnp.where(keep, sp, 0.0)
            hi = sp.astype(_bf16)
            lo = (sp - hi.astype(_f32)).astype(_bf16)
            suffix.append(_dot(jnp.concatenate([hi, lo], axis=1), tri2))
        for h in heads:
            a = jnp.exp2(z[h] - suffix[h] - run_ref[h])
            if masked:
                a = jnp.where(keep, a, 0.0)
            acc_ref[h] += _dot(a.astype(_bf16), vblk)
            run_ref[h] += suffix[h][:, 0:1]

    key_block(qi, True)

    def more(carry):
        kb, run_min = carry
        return jnp.logical_and(kb >= 0, run_min < UNDERFLOW_LOG2)

    def body(carry):
        kb, _ = carry
        key_block(kb, False)
        return kb - 1, jnp.min(run_ref[...])

    lax.while_loop(more, body, (qi - 1, jnp.min(run_ref[...])))

    o_ref[0] = jnp.where(lane < HEAD_DIM, acc_ref[0], acc_ref[1]).astype(o_ref.dtype)


def _attention(q, k, v):
    b, s, w = q.shape
    assert Q_TILE == K_TILE and s % Q_TILE == 0 and w % LANES == 0
    seq = pl.BlockSpec((1, s, LANES), lambda i, h, j: (i, 0, h))
    tile = pl.BlockSpec((1, Q_TILE, LANES), lambda i, h, j: (i, j, h))
    return pl.pallas_call(
        _attn_kernel,
        grid=(b, w // LANES, s // Q_TILE),
        in_specs=[tile, seq, seq],
        out_specs=tile,
        out_shape=jax.ShapeDtypeStruct((b, s, w), _bf16),
        scratch_shapes=[
            pltpu.VMEM((HEADS_PER_TILE, Q_TILE, LANES), _bf16),
            pltpu.VMEM((HEADS_PER_TILE, Q_TILE, LANES), _f32),
            pltpu.VMEM((HEADS_PER_TILE, Q_TILE, 1), _f32),
        ],
        compiler_params=pltpu.CompilerParams(
            dimension_semantics=("arbitrary", "arbitrary", "arbitrary"),
            vmem_limit_bytes=VMEM_LIMIT),
        name="attention",
    )(q, k, v)


def _mixer_out_kernel(x_ref, o_ref, gc_ref, sgb_ref, mod_ref, gains_ref,
                      wpa_ref, wout_ref, w1_ref, w2_ref, out_ref):
    d = x_ref.shape[-1]
    mod = lambda i: mod_ref[0, :, i * d:(i + 1) * d]
    gate1, shift2, scale2, gate2 = mod(2), mod(3), mod(4), mod(5)
    g_post_mix = gains_ref[0:1, :]
    g_pre_mlp = gains_ref[1:2, :]
    g_post_mlp = gains_ref[2:3, :]

    y_attn = _dot(o_ref[0], wpa_ref[...])
    merged = gc_ref[0] + sgb_ref[0] * y_attn
    mix = _dot(merged.astype(_bf16), wout_ref[...])
    x1 = x_ref[0] + gate1 * _rms(mix, g_post_mix)

    h2 = (_rms(x1, g_pre_mlp) * (1.0 + scale2) + shift2).astype(_bf16)
    ff = jnp.zeros_like(x1)
    for j in range(0, w1_ref.shape[1], FF_CHUNK):
        hid = jnp.maximum(_dot(h2, w1_ref[:, j:j + FF_CHUNK]), 0.0)
        ff = ff + _dot((hid * hid).astype(_bf16), w2_ref[j:j + FF_CHUNK, :])
    out_ref[0] = x1 + gate2 * _rms(ff, g_post_mlp)


def _mixer_out(x, o, gc, sgb, mod, gains, w_proj_attn, w_out, w1, w2):
    b, s, d = x.shape
    tm = SEQ_TILE
    tile = lambda width: pl.BlockSpec((1, tm, width), lambda i, j: (i, j, 0))
    return pl.pallas_call(
        _mixer_out_kernel,
        grid=(b, s // tm),
        in_specs=[
            tile(d), tile(ATTN_WIDTH), tile(d), tile(d),
            pl.BlockSpec((1, 1, N_MOD * d), lambda i, j: (i, 0, 0)),
            _const_spec(gains.shape),
            _const_spec(w_proj_attn.shape),
            _const_spec(w_out.shape),
            _const_spec(w1.shape),
            _const_spec(w2.shape),
        ],
        out_specs=tile(d),
        out_shape=jax.ShapeDtypeStruct((b, s, d), _f32),
        compiler_params=pltpu.CompilerParams(
            dimension_semantics=("arbitrary", "arbitrary"), vmem_limit_bytes=VMEM_LIMIT),
        name="mixer_out",
    )(x, o, gc, sgb, mod.reshape(b, 1, N_MOD * d), gains, w_proj_attn, w_out, w1, w2)


def kernel(x, c, w_ada, b_ada, g_pre_mix, g_post_mix, g_pre_mlp, g_post_mlp, w_in, conv_w,
           w_proj_conv, w_proj_attn, w_out, w_mlp_in, w_mlp_out):
    depth = w_ada.shape[0]
    mod = _ada(c, w_ada, b_ada)
    for l in range(depth):
        q, k, v, gc, sgb = _mixer_in(x, mod[l], g_pre_mix[l], w_in[l].astype(_bf16), conv_w[l],
                                     w_proj_conv[l].astype(_bf16))
        o = _attention(q, k, v)
        gains = jnp.stack([g_post_mix[l], g_pre_mlp[l], g_post_mlp[l]])
        x = _mixer_out(x, o, gc, sgb, mod[l], gains, w_proj_attn[l].astype(_bf16),
                       w_out[l].astype(_bf16), w_mlp_in[l].astype(_bf16),
                       w_mlp_out[l].astype(_bf16))
    return x
```

```python
import functools
import math

import jax
import jax.numpy as jnp
from jax import lax
from jax.experimental import pallas as pl
from jax.experimental.pallas import tpu as pltpu

CONV_WIDTH = 512
CONV_TAPS = 3
ATTN_WIDTH = 512
HEAD_DIM = 64
N_MOD = 6
EPS = 1e-6

LANES = 128
HEADS_PER_TILE = LANES // HEAD_DIM
ATTN_LANE_TILES = 2
SEQ_TILE = 512
Q_TILE = 256
K_TILE = 256
FF_CHUNK = 1024
VMEM_LIMIT = 56 * 1024 * 1024
LOG2E = math.log2(math.e)
UNDERFLOW_LOG2 = 160.0
Z_CLAMP = 120.0

_f32 = jnp.float32
_bf16 = jnp.bfloat16


def _dot(a, b):
    return jnp.dot(a, b, preferred_element_type=_f32)


def _rms(x, gain):
    return x * lax.rsqrt(jnp.mean(x * x, axis=-1, keepdims=True) + EPS) * gain


def _const_spec(shape):
    return pl.BlockSpec(shape, lambda *_: (0,) * len(shape), pipeline_mode=pl.Buffered(1))


def _ada_kernel(c_ref, w_ref, b_ref, o_ref):
    o_ref[0] = jnp.dot(c_ref[...], w_ref[0], preferred_element_type=_f32,
                       precision=lax.Precision.HIGHEST) + b_ref[0, 0]


def _ada(c, w_ada, b_ada):
    depth, d, _ = w_ada.shape
    b = c.shape[0]
    return pl.pallas_call(
        _ada_kernel,
        grid=(depth, N_MOD),
        in_specs=[
            pl.BlockSpec((b, d), lambda l, j: (0, 0)),
            pl.BlockSpec((1, d, d), lambda l, j: (l, 0, j)),
            pl.BlockSpec((1, 1, 1, d), lambda l, j: (l, j, 0, 0)),
        ],
        out_specs=pl.BlockSpec((1, b, d), lambda l, j: (l, 0, j)),
        out_shape=jax.ShapeDtypeStruct((depth, b, N_MOD * d), _f32),
        compiler_params=pltpu.CompilerParams(
            dimension_semantics=("arbitrary", "arbitrary"), vmem_limit_bytes=VMEM_LIMIT),
        name="ada",
    )(c, w_ada, b_ada.reshape(depth, N_MOD, 1, d))


def _mixer_in_kernel(x_ref, mod_ref, g_ref, w_in_ref, cw_ref, wpc_ref,
                     q_ref, k_ref, v_ref, gc_ref, sgb_ref, carry_ref):
    d = x_ref.shape[-1]
    tm = x_ref.shape[1]
    cw = CONV_WIDTH

    @pl.when(pl.program_id(1) == 0)
    def _():
        carry_ref[...] = jnp.zeros_like(carry_ref)

    shift = mod_ref[0, :, 0:d]
    scale = mod_ref[0, :, d:2 * d]
    x = x_ref[0]
    h = (_rms(x, g_ref[...]) * (1.0 + scale) + shift).astype(_bf16)

    bg = _dot(h, w_in_ref[:, 0:cw])
    cg = _dot(h, w_in_ref[:, cw:2 * cw])
    u = _dot(h, w_in_ref[:, 2 * cw:3 * cw])
    vv = cg * u
    row = lax.broadcasted_iota(jnp.int32, vv.shape, 0)
    prev1 = carry_ref[1:2, :]
    prev2 = carry_ref[0:1, :]
    v1 = jnp.where(row == 0, prev1, pltpu.roll(vv, 1, axis=0))
    v2 = jnp.where(row == 0, prev2, jnp.where(row == 1, prev1, pltpu.roll(vv, 2, axis=0)))
    carry_ref[...] = vv[tm - 2:tm, :]
    conv = cw_ref[0:1, :] * v2 + cw_ref[1:2, :] * v1 + cw_ref[2:3, :] * vv
    y_conv = _dot((bg * conv).astype(_bf16), wpc_ref[...])

    o = 3 * cw
    aw = ATTN_WIDTH
    q_ref[0] = (_dot(h, w_in_ref[:, o:o + aw]) * (LOG2E / math.sqrt(HEAD_DIM))).astype(_bf16)
    k_ref[0] = _dot(h, w_in_ref[:, o + aw:o + 2 * aw]).astype(_bf16)
    v_ref[0] = _dot(h, w_in_ref[:, o + 2 * aw:o + 3 * aw]).astype(_bf16)
    o += 3 * aw
    ga = _dot(h, w_in_ref[:, o:o + d])
    gc_ref[0] = jax.nn.sigmoid(ga) * y_conv
    gb = _dot(h, w_in_ref[:, o + d:o + 2 * d])
    sgb_ref[0] = jax.nn.sigmoid(gb)


def _mixer_in(x, mod, g_pre, w_in, conv_w, w_proj_conv):
    b, s, d = x.shape
    tm = SEQ_TILE
    tile = lambda width: pl.BlockSpec((1, tm, width), lambda i, j: (i, j, 0))
    return pl.pallas_call(
        _mixer_in_kernel,
        grid=(b, s // tm),
        in_specs=[
            tile(d),
            pl.BlockSpec((1, 1, N_MOD * d), lambda i, j: (i, 0, 0)),
            _const_spec((1, d)),
            _const_spec(w_in.shape),
            _const_spec(conv_w.shape),
            _const_spec(w_proj_conv.shape),
        ],
        out_specs=[tile(ATTN_WIDTH), tile(ATTN_WIDTH), tile(ATTN_WIDTH), tile(d), tile(d)],
        out_shape=[
            jax.ShapeDtypeStruct((b, s, ATTN_WIDTH), _bf16),
            jax.ShapeDtypeStruct((b, s, ATTN_WIDTH), _bf16),
            jax.ShapeDtypeStruct((b, s, ATTN_WIDTH), _bf16),
            jax.ShapeDtypeStruct((b, s, d), _f32),
            jax.ShapeDtypeStruct((b, s, d), _f32),
        ],
        scratch_shapes=[pltpu.VMEM((CONV_TAPS - 1, CONV_WIDTH), _f32)],
        compiler_params=pltpu.CompilerParams(
            dimension_semantics=("arbitrary", "arbitrary"), vmem_limit_bytes=VMEM_LIMIT),
        name="mixer_in",
    )(x, mod.reshape(b, 1, N_MOD * d), g_pre.reshape(1, d), w_in, conv_w, w_proj_conv)


def _attn_kernel(q_ref, k_ref, v_ref, o_ref, qh_ref, acc_ref, run_ref):
    tq = q_ref.shape[1]
    qi = pl.program_id(2)
    n_heads = qh_ref.shape[0]
    lanes_of = lambda h: slice((h // HEADS_PER_TILE) * LANES, (h // HEADS_PER_TILE + 1) * LANES)

    lane = lax.broadcasted_iota(jnp.int32, (tq, LANES), 1)
    own = [lane < HEAD_DIM, lane >= HEAD_DIM]
    for h in range(n_heads):
        q = q_ref[0, :, lanes_of(h)]
        qh_ref[h] = jnp.where(own[h % HEADS_PER_TILE], q, jnp.zeros_like(q))
    acc_ref[...] = jnp.zeros_like(acc_ref)
    run_ref[...] = jnp.zeros_like(run_ref)

    tri2 = ((lax.broadcasted_iota(jnp.int32, (2 * K_TILE, K_TILE), 0) & (K_TILE - 1))
            >= lax.broadcasted_iota(jnp.int32, (2 * K_TILE, K_TILE), 1)).astype(_bf16)

    def key_block(kb, masked):
        start = pl.multiple_of(kb * K_TILE, K_TILE)
        if masked:
            keep = (lax.broadcasted_iota(jnp.int32, (tq, K_TILE), 1)
                    < lax.broadcasted_iota(jnp.int32, (tq, K_TILE), 0))
        heads = range(n_heads)
        z = [lax.dot_general(qh_ref[h], k_ref[0, pl.ds(start, K_TILE), lanes_of(h)],
                             (((1,), (1,)), ((), ())),
                             preferred_element_type=_f32) for h in heads]
        suffix = []
        for h in heads:
            sp = jnp.maximum(jnp.log2(1.0 + jnp.exp2(jnp.minimum(z[h], Z_CLAMP))), z[h])
            if masked:
                sp = jnp.where(keep, sp, 0.0)
            hi = sp.astype(_bf16)
            lo = (sp - hi.astype(_f32)).astype(_bf16)
            suffix.append(_dot(jnp.concatenate([hi, lo], axis=1), tri2))
        for h in heads:
            a = jnp.exp2(z[h] - suffix[h] - run_ref[h])
            if masked:
                a = jnp.where(keep, a, 0.0)
            acc_ref[h] += _dot(a.astype(_bf16), v_ref[0, pl.ds(start, K_TILE), lanes_of(h)])
            run_ref[h] += suffix[h][:, 0:1]

    key_block(qi, True)

    def more(carry):
        kb, run_min = carry
        return jnp.logical_and(kb >= 0, run_min < UNDERFLOW_LOG2)

    def body(carry):
        kb, _ = carry
        key_block(kb, False)
        return kb - 1, jnp.min(run_ref[...])

    lax.while_loop(more, body, (qi - 1, jnp.min(run_ref[...])))

    for h in range(0, n_heads, HEADS_PER_TILE):
        o_ref[0, :, lanes_of(h)] = jnp.where(own[0], acc_ref[h], acc_ref[h + 1]).astype(o_ref.dtype)


def _attention(q, k, v):
    b, s, w = q.shape
    width = ATTN_LANE_TILES * LANES
    n_heads = ATTN_LANE_TILES * HEADS_PER_TILE
    assert Q_TILE == K_TILE and s % Q_TILE == 0 and w % width == 0
    seq = pl.BlockSpec((1, s, width), lambda i, h, j: (i, 0, h))
    tile = pl.BlockSpec((1, Q_TILE, width), lambda i, h, j: (i, j, h))
    return pl.pallas_call(
        _attn_kernel,
        grid=(b, w // width, s // Q_TILE),
        in_specs=[tile, seq, seq],
        out_specs=tile,
        out_shape=jax.ShapeDtypeStruct((b, s, w), _bf16),
        scratch_shapes=[
            pltpu.VMEM((n_heads, Q_TILE, LANES), _bf16),
            pltpu.VMEM((n_heads, Q_TILE, LANES), _f32),
            pltpu.VMEM((n_heads, Q_TILE, 1), _f32),
        ],
        compiler_params=pltpu.CompilerParams(
            dimension_semantics=("arbitrary", "arbitrary", "arbitrary"),
            vmem_limit_bytes=VMEM_LIMIT),
        name="attention",
    )(q, k, v)


def _mixer_out_kernel(x_ref, o_ref, gc_ref, sgb_ref, mod_ref, gains_ref,
                      wpa_ref, wout_ref, w1_ref, w2_ref, out_ref):
    d = x_ref.shape[-1]
    mod = lambda i: mod_ref[0, :, i * d:(i + 1) * d]
    gate1, shift2, scale2, gate2 = mod(2), mod(3), mod(4), mod(5)
    g_post_mix = gains_ref[0:1, :]
    g_pre_mlp = gains_ref[1:2, :]
    g_post_mlp = gains_ref[2:3, :]

    y_attn = _dot(o_ref[0], wpa_ref[...])
    merged = gc_ref[0] + sgb_ref[0] * y_attn
    mix = _dot(merged.astype(_bf16), wout_ref[...])
    x1 = x_ref[0] + gate1 * _rms(mix, g_post_mix)

    h2 = (_rms(x1, g_pre_mlp) * (1.0 + scale2) + shift2).astype(_bf16)
    ff = jnp.zeros_like(x1)
    for j in range(0, w1_ref.shape[1], FF_CHUNK):
        hid = jnp.maximum(_dot(h2, w1_ref[:, j:j + FF_CHUNK]), 0.0)
        ff = ff + _dot((hid * hid).astype(_bf16), w2_ref[j:j + FF_CHUNK, :])
    out_ref[0] = x1 + gate2 * _rms(ff, g_post_mlp)


def _mixer_out(x, o, gc, sgb, mod, gains, w_proj_attn, w_out, w1, w2):
    b, s, d = x.shape
    tm = SEQ_TILE
    tile = lambda width: pl.BlockSpec((1, tm, width), lambda i, j: (i, j, 0))
    return pl.pallas_call(
        _mixer_out_kernel,
        grid=(b, s // tm),
        in_specs=[
            tile(d), tile(ATTN_WIDTH), tile(d), tile(d),
            pl.BlockSpec((1, 1, N_MOD * d), lambda i, j: (i, 0, 0)),
            _const_spec(gains.shape),
            _const_spec(w_proj_attn.shape),
            _const_spec(w_out.shape),
            _const_spec(w1.shape),
            _const_spec(w2.shape),
        ],
        out_specs=tile(d),
        out_shape=jax.ShapeDtypeStruct((b, s, d), _f32),
        compiler_params=pltpu.CompilerParams(
            dimension_semantics=("arbitrary", "arbitrary"), vmem_limit_bytes=VMEM_LIMIT),
        name="mixer_out",
    )(x, o, gc, sgb, mod.reshape(b, 1, N_MOD * d), gains, w_proj_attn, w_out, w1, w2)


def kernel(x, c, w_ada, b_ada, g_pre_mix, g_post_mix, g_pre_mlp, g_post_mlp, w_in, conv_w,
           w_proj_conv, w_proj_attn, w_out, w_mlp_in, w_mlp_out):
    depth = w_ada.shape[0]
    mod = _ada(c, w_ada, b_ada)
    for l in range(depth):
        q, k, v, gc, sgb = _mixer_in(x, mod[l], g_pre_mix[l], w_in[l].astype(_bf16), conv_w[l],
                                     w_proj_conv[l].astype(_bf16))
        o = _attention(q, k, v)
        gains = jnp.stack([g_post_mix[l], g_pre_mlp[l], g_post_mlp[l]])
        x = _mixer_out(x, o, gc, sgb, mod[l], gains, w_proj_attn[l].astype(_bf16),
                       w_out[l].astype(_bf16), w_mlp_in[l].astype(_bf16),
                       w_mlp_out[l].astype(_bf16))
    return x
```

```python
import functools
import math

import jax
import jax.numpy as jnp
from jax import lax
from jax.experimental import pallas as pl
from jax.experimental.pallas import tpu as pltpu

CONV_WIDTH = 512
CONV_TAPS = 3
ATTN_WIDTH = 512
HEAD_DIM = 64
N_MOD = 6
EPS = 1e-6

LANES = 128
HEADS_PER_TILE = LANES // HEAD_DIM
ATTN_LANE_TILES = 4
SEQ_TILE = 512
Q_TILE = 256
K_TILE = 256
FF_CHUNK = 1024
VMEM_LIMIT = 56 * 1024 * 1024
LOG2E = math.log2(math.e)
UNDERFLOW_LOG2 = 152.0
Z_CLAMP = 120.0

_f32 = jnp.float32
_bf16 = jnp.bfloat16


def _dot(a, b):
    return jnp.dot(a, b, preferred_element_type=_f32)


def _rms(x, gain):
    return x * lax.rsqrt(jnp.mean(x * x, axis=-1, keepdims=True) + EPS) * gain


def _const_spec(shape):
    return pl.BlockSpec(shape, lambda *_: (0,) * len(shape), pipeline_mode=pl.Buffered(1))


def _ada_kernel(c_ref, w_ref, b_ref, o_ref):
    o_ref[0] = jnp.dot(c_ref[...], w_ref[0], preferred_element_type=_f32,
                       precision=lax.Precision.HIGHEST) + b_ref[0, 0]


def _ada(c, w_ada, b_ada):
    depth, d, _ = w_ada.shape
    b = c.shape[0]
    return pl.pallas_call(
        _ada_kernel,
        grid=(depth, N_MOD),
        in_specs=[
            pl.BlockSpec((b, d), lambda l, j: (0, 0)),
            pl.BlockSpec((1, d, d), lambda l, j: (l, 0, j)),
            pl.BlockSpec((1, 1, 1, d), lambda l, j: (l, j, 0, 0)),
        ],
        out_specs=pl.BlockSpec((1, b, d), lambda l, j: (l, 0, j)),
        out_shape=jax.ShapeDtypeStruct((depth, b, N_MOD * d), _f32),
        compiler_params=pltpu.CompilerParams(
            dimension_semantics=("arbitrary", "arbitrary"), vmem_limit_bytes=VMEM_LIMIT),
        name="ada",
    )(c, w_ada, b_ada.reshape(depth, N_MOD, 1, d))


def _mixer_in_kernel(x_ref, mod_ref, g_ref, w_in_ref, cw_ref, wpc_ref,
                     q_ref, k_ref, v_ref, gc_ref, sgb_ref, carry_ref):
    d = x_ref.shape[-1]
    tm = x_ref.shape[1]
    cw = CONV_WIDTH

    @pl.when(pl.program_id(1) == 0)
    def _():
        carry_ref[...] = jnp.zeros_like(carry_ref)

    shift = mod_ref[0, :, 0:d]
    scale = mod_ref[0, :, d:2 * d]
    x = x_ref[0]
    h = (_rms(x, g_ref[...]) * (1.0 + scale) + shift).astype(_bf16)

    bg = _dot(h, w_in_ref[:, 0:cw])
    cg = _dot(h, w_in_ref[:, cw:2 * cw])
    u = _dot(h, w_in_ref[:, 2 * cw:3 * cw])
    vv = cg * u
    row = lax.broadcasted_iota(jnp.int32, vv.shape, 0)
    prev1 = carry_ref[1:2, :]
    prev2 = carry_ref[0:1, :]
    v1 = jnp.where(row == 0, prev1, pltpu.roll(vv, 1, axis=0))
    v2 = jnp.where(row == 0, prev2, jnp.where(row == 1, prev1, pltpu.roll(vv, 2, axis=0)))
    carry_ref[...] = vv[tm - 2:tm, :]
    conv = cw_ref[0:1, :] * v2 + cw_ref[1:2, :] * v1 + cw_ref[2:3, :] * vv
    y_conv = _dot((bg * conv).astype(_bf16), wpc_ref[...])

    o = 3 * cw
    aw = ATTN_WIDTH
    q_ref[0] = (_dot(h, w_in_ref[:, o:o + aw]) * (LOG2E / math.sqrt(HEAD_DIM))).astype(_bf16)
    k_ref[0] = _dot(h, w_in_ref[:, o + aw:o + 2 * aw]).astype(_bf16)
    v_ref[0] = _dot(h, w_in_ref[:, o + 2 * aw:o + 3 * aw]).astype(_bf16)
    o += 3 * aw
    ga = _dot(h, w_in_ref[:, o:o + d])
    gc_ref[0] = jax.nn.sigmoid(ga) * y_conv
    gb = _dot(h, w_in_ref[:, o + d:o + 2 * d])
    sgb_ref[0] = jax.nn.sigmoid(gb)


def _mixer_in(x, mod, g_pre, w_in, conv_w, w_proj_conv):
    b, s, d = x.shape
    tm = SEQ_TILE
    tile = lambda width: pl.BlockSpec((1, tm, width), lambda i, j: (i, j, 0))
    return pl.pallas_call(
        _mixer_in_kernel,
        grid=(b, s // tm),
        in_specs=[
            tile(d),
            pl.BlockSpec((1, 1, N_MOD * d), lambda i, j: (i, 0, 0)),
            _const_spec((1, d)),
            _const_spec(w_in.shape),
            _const_spec(conv_w.shape),
            _const_spec(w_proj_conv.shape),
        ],
        out_specs=[tile(ATTN_WIDTH), tile(ATTN_WIDTH), tile(ATTN_WIDTH), tile(d), tile(d)],
        out_shape=[
            jax.ShapeDtypeStruct((b, s, ATTN_WIDTH), _bf16),
            jax.ShapeDtypeStruct((b, s, ATTN_WIDTH), _bf16),
            jax.ShapeDtypeStruct((b, s, ATTN_WIDTH), _bf16),
            jax.ShapeDtypeStruct((b, s, d), _f32),
            jax.ShapeDtypeStruct((b, s, d), _f32),
        ],
        scratch_shapes=[pltpu.VMEM((CONV_TAPS - 1, CONV_WIDTH), _f32)],
        compiler_params=pltpu.CompilerParams(
            dimension_semantics=("arbitrary", "arbitrary"), vmem_limit_bytes=VMEM_LIMIT),
        name="mixer_in",
    )(x, mod.reshape(b, 1, N_MOD * d), g_pre.reshape(1, d), w_in, conv_w, w_proj_conv)


def _attn_kernel(q_ref, k_ref, v_ref, o_ref, qh_ref, acc_ref, run_ref):
    tq = q_ref.shape[1]
    qi = pl.program_id(2)
    n_heads = qh_ref.shape[0]
    lanes_of = lambda h: slice((h // HEADS_PER_TILE) * LANES, (h // HEADS_PER_TILE + 1) * LANES)

    lane = lax.broadcasted_iota(jnp.int32, (tq, LANES), 1)
    own = [lane < HEAD_DIM, lane >= HEAD_DIM]
    for h in range(n_heads):
        q = q_ref[0, :, lanes_of(h)]
        qh_ref[h] = jnp.where(own[h % HEADS_PER_TILE], q, jnp.zeros_like(q))
    acc_ref[...] = jnp.zeros_like(acc_ref)
    run_ref[...] = jnp.zeros_like(run_ref)

    tri2 = (lax.broadcasted_iota(jnp.int32, (K_TILE, K_TILE), 0)
            >= lax.broadcasted_iota(jnp.int32, (K_TILE, K_TILE), 1)).astype(_bf16)

    def key_block(kb, masked):
        start = pl.multiple_of(kb * K_TILE, K_TILE)
        if masked:
            keep = (lax.broadcasted_iota(jnp.int32, (tq, K_TILE), 1)
                    < lax.broadcasted_iota(jnp.int32, (tq, K_TILE), 0))
        heads = range(n_heads)
        z = [lax.dot_general(qh_ref[h], k_ref[0, pl.ds(start, K_TILE), lanes_of(h)],
                             (((1,), (1,)), ((), ())),
                             preferred_element_type=_f32) for h in heads]
        suffix = []
        for h in heads:
            sp = jnp.maximum(jnp.log2(1.0 + jnp.exp2(jnp.minimum(z[h], Z_CLAMP))), z[h])
            if masked:
                sp = jnp.where(keep, sp, 0.0)
            suffix.append(_dot(sp.astype(_bf16), tri2))
        for h in heads:
            run = run_ref[h]
            a = jnp.exp2(z[h] - suffix[h] - jnp.concatenate([run] * (K_TILE // LANES), axis=1))
            if masked:
                a = jnp.where(keep, a, 0.0)
            acc_ref[h] += _dot(a.astype(_bf16), v_ref[0, pl.ds(start, K_TILE), lanes_of(h)])
            run_ref[h] = run + jnp.broadcast_to(suffix[h][:, 0:1], run.shape)

    key_block(qi, True)

    def more(carry):
        kb, run_min = carry
        return jnp.logical_and(kb >= 0, run_min < UNDERFLOW_LOG2)

    def body(carry):
        kb, _ = carry
        key_block(kb, False)
        return kb - 1, jnp.min(run_ref[...])

    lax.while_loop(more, body, (qi - 1, jnp.min(run_ref[...])))

    for h in range(0, n_heads, HEADS_PER_TILE):
        o_ref[0, :, lanes_of(h)] = jnp.where(own[0], acc_ref[h], acc_ref[h + 1]).astype(o_ref.dtype)


def _attention(q, k, v):
    b, s, w = q.shape
    width = ATTN_LANE_TILES * LANES
    n_heads = ATTN_LANE_TILES * HEADS_PER_TILE
    assert Q_TILE == K_TILE and s % Q_TILE == 0 and w % width == 0
    seq = pl.BlockSpec((1, s, width), lambda i, h, j: (i, 0, h))
    tile = pl.BlockSpec((1, Q_TILE, width), lambda i, h, j: (i, j, h))
    return pl.pallas_call(
        _attn_kernel,
        grid=(b, w // width, s // Q_TILE),
        in_specs=[tile, seq, seq],
        out_specs=tile,
        out_shape=jax.ShapeDtypeStruct((b, s, w), _bf16),
        scratch_shapes=[
            pltpu.VMEM((n_heads, Q_TILE, LANES), _bf16),
            pltpu.VMEM((n_heads, Q_TILE, LANES), _f32),
            pltpu.VMEM((n_heads, Q_TILE, LANES), _f32),
        ],
        compiler_params=pltpu.CompilerParams(
            dimension_semantics=("arbitrary", "arbitrary", "arbitrary"),
            vmem_limit_bytes=VMEM_LIMIT),
        name="attention",
    )(q, k, v)


def _mixer_out_kernel(x_ref, o_ref, gc_ref, sgb_ref, mod_ref, gains_ref,
                      wpa_ref, wout_ref, w1_ref, w2_ref, out_ref):
    d = x_ref.shape[-1]
    mod = lambda i: mod_ref[0, :, i * d:(i + 1) * d]
    gate1, shift2, scale2, gate2 = mod(2), mod(3), mod(4), mod(5)
    g_post_mix = gains_ref[0:1, :]
    g_pre_mlp = gains_ref[1:2, :]
    g_post_mlp = gains_ref[2:3, :]

    y_attn = _dot(o_ref[0], wpa_ref[...])
    merged = gc_ref[0] + sgb_ref[0] * y_attn
    mix = _dot(merged.astype(_bf16), wout_ref[...])
    x1 = x_ref[0] + gate1 * _rms(mix, g_post_mix)

    h2 = (_rms(x1, g_pre_mlp) * (1.0 + scale2) + shift2).astype(_bf16)
    ff = jnp.zeros_like(x1)
    for j in range(0, w1_ref.shape[1], FF_CHUNK):
        hid = jnp.maximum(_dot(h2, w1_ref[:, j:j + FF_CHUNK]), 0.0)
        ff = ff + _dot((hid * hid).astype(_bf16), w2_ref[j:j + FF_CHUNK, :])
    out_ref[0] = x1 + gate2 * _rms(ff, g_post_mlp)


def _mixer_out(x, o, gc, sgb, mod, gains, w_proj_attn, w_out, w1, w2):
    b, s, d = x.shape
    tm = SEQ_TILE
    tile = lambda width: pl.BlockSpec((1, tm, width), lambda i, j: (i, j, 0))
    return pl.pallas_call(
        _mixer_out_kernel,
        grid=(b, s // tm),
        in_specs=[
            tile(d), tile(ATTN_WIDTH), tile(d), tile(d),
            pl.BlockSpec((1, 1, N_MOD * d), lambda i, j: (i, 0, 0)),
            _const_spec(gains.shape),
            _const_spec(w_proj_attn.shape),
            _const_spec(w_out.shape),
            _const_spec(w1.shape),
            _const_spec(w2.shape),
        ],
        out_specs=tile(d),
        out_shape=jax.ShapeDtypeStruct((b, s, d), _f32),
        compiler_params=pltpu.CompilerParams(
            dimension_semantics=("arbitrary", "arbitrary"), vmem_limit_bytes=VMEM_LIMIT),
        name="mixer_out",
    )(x, o, gc, sgb, mod.reshape(b, 1, N_MOD * d), gains, w_proj_attn, w_out, w1, w2)


def kernel(x, c, w_ada, b_ada, g_pre_mix, g_post_mix, g_pre_mlp, g_post_mlp, w_in, conv_w,
           w_proj_conv, w_proj_attn, w_out, w_mlp_in, w_mlp_out):
    depth = w_ada.shape[0]
    mod = _ada(c, w_ada, b_ada)
    for l in range(depth):
        q, k, v, gc, sgb = _mixer_in(x, mod[l], g_pre_mix[l], w_in[l].astype(_bf16), conv_w[l],
                                     w_proj_conv[l].astype(_bf16))
        o = _attention(q, k, v)
        gains = jnp.stack([g_post_mix[l], g_pre_mlp[l], g_post_mlp[l]])
        x = _mixer_out(x, o, gc, sgb, mod[l], gains, w_proj_attn[l].astype(_bf16),
                       w_out[l].astype(_bf16), w_mlp_in[l].astype(_bf16),
                       w_mlp_out[l].astype(_bf16))
    return x
```

```python
import functools
import math

import jax
import jax.numpy as jnp
from jax import lax
from jax.experimental import pallas as pl
from jax.experimental.pallas import tpu as pltpu

CONV_WIDTH = 512
CONV_TAPS = 3
ATTN_WIDTH = 512
HEAD_DIM = 64
N_MOD = 6
EPS = 1e-6

LANES = 128
HEADS_PER_TILE = LANES // HEAD_DIM
ATTN_LANE_TILES = 4
SEQ_TILE = 512
Q_TILE = 256
K_TILE = 256
FF_CHUNK = 1024
ROW_SUBTILES = 2
VMEM_LIMIT = 56 * 1024 * 1024
LOG2E = math.log2(math.e)
UNDERFLOW_LOG2 = 152.0
Z_CLAMP = 120.0

_f32 = jnp.float32
_bf16 = jnp.bfloat16


def _dot(a, b):
    return jnp.dot(a, b, preferred_element_type=_f32)


def _rms(x, gain):
    return x * lax.rsqrt(jnp.mean(x * x, axis=-1, keepdims=True) + EPS) * gain


def _const_spec(shape):
    return pl.BlockSpec(shape, lambda *_: (0,) * len(shape), pipeline_mode=pl.Buffered(1))


def _ada_kernel(ct_ref, w_ref, b_ref, o_ref):
    w = w_ref[0]
    rows = [jnp.sum(w * ct_ref[:, i:i + 1], axis=0, keepdims=True) for i in range(ct_ref.shape[1])]
    o_ref[0] = jnp.concatenate(rows, axis=0) + b_ref[0, 0]


def _ada(c, w_ada, b_ada):
    depth, d, _ = w_ada.shape
    b = c.shape[0]
    return pl.pallas_call(
        _ada_kernel,
        grid=(depth, N_MOD),
        in_specs=[
            pl.BlockSpec((d, b), lambda l, j: (0, 0)),
            pl.BlockSpec((1, d, d), lambda l, j: (l, 0, j)),
            pl.BlockSpec((1, 1, 1, d), lambda l, j: (l, j, 0, 0)),
        ],
        out_specs=pl.BlockSpec((1, b, d), lambda l, j: (l, 0, j)),
        out_shape=jax.ShapeDtypeStruct((depth, b, N_MOD * d), _f32),
        compiler_params=pltpu.CompilerParams(
            dimension_semantics=("arbitrary", "arbitrary"), vmem_limit_bytes=VMEM_LIMIT),
        name="ada",
    )(c.T, w_ada, b_ada.reshape(depth, N_MOD, 1, d))


def _mixer_in_kernel(x_ref, mod_ref, g_ref, w_in_ref, cw_ref, wpc_ref,
                     q_ref, k_ref, v_ref, gc_ref, sgb_ref, carry_ref):
    d = x_ref.shape[-1]
    tm = x_ref.shape[1]
    cw = CONV_WIDTH

    @pl.when(pl.program_id(1) == 0)
    def _():
        carry_ref[...] = jnp.zeros_like(carry_ref)

    shift = mod_ref[0, :, 0:d]
    scale = mod_ref[0, :, d:2 * d]
    x = x_ref[0]
    h = (_rms(x, g_ref[...]) * (1.0 + scale) + shift).astype(_bf16)

    bg = _dot(h, w_in_ref[:, 0:cw])
    cg = _dot(h, w_in_ref[:, cw:2 * cw])
    u = _dot(h, w_in_ref[:, 2 * cw:3 * cw])

    o = 3 * cw
    aw = ATTN_WIDTH
    q_ref[0] = (_dot(h, w_in_ref[:, o:o + aw]) * (LOG2E / math.sqrt(HEAD_DIM))).astype(_bf16)
    k_ref[0] = _dot(h, w_in_ref[:, o + aw:o + 2 * aw]).astype(_bf16)
    v_ref[0] = _dot(h, w_in_ref[:, o + 2 * aw:o + 3 * aw]).astype(_bf16)
    o += 3 * aw
    gb = _dot(h, w_in_ref[:, o + d:o + 2 * d])
    sgb_ref[0] = jax.nn.sigmoid(gb)
    ga = _dot(h, w_in_ref[:, o:o + d])

    vv = cg * u
    row = lax.broadcasted_iota(jnp.int32, vv.shape, 0)
    prev1 = carry_ref[1:2, :]
    prev2 = carry_ref[0:1, :]
    v1 = jnp.where(row == 0, prev1, pltpu.roll(vv, 1, axis=0))
    v2 = jnp.where(row == 0, prev2, jnp.where(row == 1, prev1, pltpu.roll(vv, 2, axis=0)))
    carry_ref[...] = vv[tm - 2:tm, :]
    conv = cw_ref[0:1, :] * v2 + cw_ref[1:2, :] * v1 + cw_ref[2:3, :] * vv
    y_conv = _dot((bg * conv).astype(_bf16), wpc_ref[...])
    gc_ref[0] = jax.nn.sigmoid(ga) * y_conv


def _mixer_in(x, mod, g_pre, w_in, conv_w, w_proj_conv):
    b, s, d = x.shape
    tm = SEQ_TILE
    tile = lambda width: pl.BlockSpec((1, tm, width), lambda i, j: (i, j, 0))
    return pl.pallas_call(
        _mixer_in_kernel,
        grid=(b, s // tm),
        in_specs=[
            tile(d),
            pl.BlockSpec((1, 1, N_MOD * d), lambda i, j: (i, 0, 0)),
            _const_spec((1, d)),
            _const_spec(w_in.shape),
            _const_spec(conv_w.shape),
            _const_spec(w_proj_conv.shape),
        ],
        out_specs=[tile(ATTN_WIDTH), tile(ATTN_WIDTH), tile(ATTN_WIDTH), tile(d), tile(d)],
        out_shape=[
            jax.ShapeDtypeStruct((b, s, ATTN_WIDTH), _bf16),
            jax.ShapeDtypeStruct((b, s, ATTN_WIDTH), _bf16),
            jax.ShapeDtypeStruct((b, s, ATTN_WIDTH), _bf16),
            jax.ShapeDtypeStruct((b, s, d), _f32),
            jax.ShapeDtypeStruct((b, s, d), _f32),
        ],
        scratch_shapes=[pltpu.VMEM((CONV_TAPS - 1, CONV_WIDTH), _f32)],
        compiler_params=pltpu.CompilerParams(
            dimension_semantics=("arbitrary", "arbitrary"), vmem_limit_bytes=VMEM_LIMIT),
        name="mixer_in",
    )(x, mod.reshape(b, 1, N_MOD * d), g_pre.reshape(1, d), w_in, conv_w, w_proj_conv)


def _attn_kernel(q_ref, k_ref, v_ref, o_ref, qh_ref, acc_ref, run_ref):
    tq = q_ref.shape[1]
    qi = pl.program_id(2)
    n_heads = qh_ref.shape[0]
    lanes_of = lambda h: slice((h // HEADS_PER_TILE) * LANES, (h // HEADS_PER_TILE + 1) * LANES)

    lane = lax.broadcasted_iota(jnp.int32, (tq, LANES), 1)
    own = [lane < HEAD_DIM, lane >= HEAD_DIM]
    for h in range(n_heads):
        q = q_ref[0, :, lanes_of(h)]
        qh_ref[h] = jnp.where(own[h % HEADS_PER_TILE], q, jnp.zeros_like(q))
    acc_ref[...] = jnp.zeros_like(acc_ref)
    run_ref[...] = jnp.zeros_like(run_ref)

    tri2 = (lax.broadcasted_iota(jnp.int32, (K_TILE, K_TILE), 0)
            >= lax.broadcasted_iota(jnp.int32, (K_TILE, K_TILE), 1)).astype(_bf16)

    def key_block(kb, masked):
        start = pl.multiple_of(kb * K_TILE, K_TILE)
        if masked:
            keep = (lax.broadcasted_iota(jnp.int32, (tq, K_TILE), 1)
                    < lax.broadcasted_iota(jnp.int32, (tq, K_TILE), 0))
        heads = range(n_heads)
        z = [lax.dot_general(qh_ref[h], k_ref[0, pl.ds(start, K_TILE), lanes_of(h)],
                             (((1,), (1,)), ((), ())),
                             preferred_element_type=_f32) for h in heads]
        suffix = []
        for h in heads:
            sp = jnp.maximum(jnp.log2(1.0 + jnp.exp2(jnp.minimum(z[h], Z_CLAMP))), z[h])
            if masked:
                sp = jnp.where(keep, sp, 0.0)
            suffix.append(_dot(sp.astype(_bf16), tri2))
        for h in heads:
            run = run_ref[h]
            a = jnp.exp2(z[h] - suffix[h] - jnp.concatenate([run] * (K_TILE // LANES), axis=1))
            if masked:
                a = jnp.where(keep, a, 0.0)
            acc_ref[h] += _dot(a.astype(_bf16), v_ref[0, pl.ds(start, K_TILE), lanes_of(h)])
            run_ref[h] = run + jnp.broadcast_to(suffix[h][:, 0:1], run.shape)

    key_block(qi, True)

    def more(carry):
        kb, run_min = carry
        return jnp.logical_and(kb >= 0, run_min < UNDERFLOW_LOG2)

    def body(carry):
        kb, _ = carry
        key_block(kb, False)
        return kb - 1, jnp.min(run_ref[...])

    lax.while_loop(more, body, (qi - 1, jnp.min(run_ref[...])))

    for h in range(0, n_heads, HEADS_PER_TILE):
        o_ref[0, :, lanes_of(h)] = jnp.where(own[0], acc_ref[h], acc_ref[h + 1]).astype(o_ref.dtype)


def _attention(q, k, v):
    b, s, w = q.shape
    width = ATTN_LANE_TILES * LANES
    n_heads = ATTN_LANE_TILES * HEADS_PER_TILE
    assert Q_TILE == K_TILE and s % Q_TILE == 0 and w % width == 0
    seq = pl.BlockSpec((1, s, width), lambda i, h, j: (i, 0, h))
    tile = pl.BlockSpec((1, Q_TILE, width), lambda i, h, j: (i, j, h))
    return pl.pallas_call(
        _attn_kernel,
        grid=(b, w // width, s // Q_TILE),
        in_specs=[tile, seq, seq],
        out_specs=tile,
        out_shape=jax.ShapeDtypeStruct((b, s, w), _bf16),
        scratch_shapes=[
            pltpu.VMEM((n_heads, Q_TILE, LANES), _bf16),
            pltpu.VMEM((n_heads, Q_TILE, LANES), _f32),
            pltpu.VMEM((n_heads, Q_TILE, LANES), _f32),
        ],
        compiler_params=pltpu.CompilerParams(
            dimension_semantics=("arbitrary", "arbitrary", "arbitrary"),
            vmem_limit_bytes=VMEM_LIMIT),
        name="attention",
    )(q, k, v)


def _mixer_out_kernel(x_ref, o_ref, gc_ref, sgb_ref, mod_ref, gains_ref,
                      wpa_ref, wout_ref, w1_ref, w2_ref, out_ref):
    d = x_ref.shape[-1]
    mod = lambda i: mod_ref[0, :, i * d:(i + 1) * d]
    gate1, shift2, scale2, gate2 = mod(2), mod(3), mod(4), mod(5)
    g_post_mix = gains_ref[0:1, :]
    g_pre_mlp = gains_ref[1:2, :]
    g_post_mlp = gains_ref[2:3, :]

    tm = x_ref.shape[1]
    subs = [pl.ds(r, tm // ROW_SUBTILES) for r in range(0, tm, tm // ROW_SUBTILES)]

    mix = []
    for rows in subs:
        y_attn = _dot(o_ref[0, rows, :], wpa_ref[...])
        merged = gc_ref[0, rows, :] + sgb_ref[0, rows, :] * y_attn
        mix.append(_dot(merged.astype(_bf16), wout_ref[...]))
    x1, h2 = [], []
    for i, rows in enumerate(subs):
        x1.append(x_ref[0, rows, :] + gate1 * _rms(mix[i], g_post_mix))
        h2.append((_rms(x1[i], g_pre_mlp) * (1.0 + scale2) + shift2).astype(_bf16))
    for i, rows in enumerate(subs):
        ff = jnp.zeros_like(x1[i])
        for j in range(0, w1_ref.shape[1], FF_CHUNK):
            hid = jnp.maximum(_dot(h2[i], w1_ref[:, j:j + FF_CHUNK]), 0.0)
            ff = ff + _dot((hid * hid).astype(_bf16), w2_ref[j:j + FF_CHUNK, :])
        out_ref[0, rows, :] = x1[i] + gate2 * _rms(ff, g_post_mlp)


def _mixer_out(x, o, gc, sgb, mod, gains, w_proj_attn, w_out, w1, w2):
    b, s, d = x.shape
    tm = SEQ_TILE
    tile = lambda width: pl.BlockSpec((1, tm, width), lambda i, j: (i, j, 0))
    return pl.pallas_call(
        _mixer_out_kernel,
        grid=(b, s // tm),
        in_specs=[
            tile(d), tile(ATTN_WIDTH), tile(d), tile(d),
            pl.BlockSpec((1, 1, N_MOD * d), lambda i, j: (i, 0, 0)),
            _const_spec(gains.shape),
            _const_spec(w_proj_attn.shape),
            _const_spec(w_out.shape),
            _const_spec(w1.shape),
            _const_spec(w2.shape),
        ],
        out_specs=tile(d),
        out_shape=jax.ShapeDtypeStruct((b, s, d), _f32),
        compiler_params=pltpu.CompilerParams(
            dimension_semantics=("arbitrary", "arbitrary"), vmem_limit_bytes=VMEM_LIMIT),
        name="mixer_out",
    )(x, o, gc, sgb, mod.reshape(b, 1, N_MOD * d), gains, w_proj_attn, w_out, w1, w2)


def kernel(x, c, w_ada, b_ada, g_pre_mix, g_post_mix, g_pre_mlp, g_post_mlp, w_in, conv_w,
           w_proj_conv, w_proj_attn, w_out, w_mlp_in, w_mlp_out):
    depth = w_ada.shape[0]
    mod = _ada(c, w_ada, b_ada)
    for l in range(depth):
        q, k, v, gc, sgb = _mixer_in(x, mod[l], g_pre_mix[l], w_in[l].astype(_bf16), conv_w[l],
                                     w_proj_conv[l].astype(_bf16))
        o = _attention(q, k, v)
        gains = jnp.stack([g_post_mix[l], g_pre_mlp[l], g_post_mlp[l]])
        x = _mixer_out(x, o, gc, sgb, mod[l], gains, w_proj_attn[l].astype(_bf16),
                       w_out[l].astype(_bf16), w_mlp_in[l].astype(_bf16),
                       w_mlp_out[l].astype(_bf16))
    return x
```

```python
import functools
import math

import jax
import jax.numpy as jnp
from jax import lax
from jax.experimental import pallas as pl
from jax.experimental.pallas import tpu as pltpu

CONV_WIDTH = 512
CONV_TAPS = 3
ATTN_WIDTH = 512
HEAD_DIM = 64
N_MOD = 6
EPS = 1e-6

LANES = 128
HEADS_PER_TILE = LANES // HEAD_DIM
ATTN_LANE_TILES = 4
SEQ_TILE = 512
Q_TILE = 256
K_TILE = 256
FF_CHUNK = 1024
ROW_SUBTILES = 2
VMEM_LIMIT = 56 * 1024 * 1024
LOG2E = math.log2(math.e)
UNDERFLOW_LOG2 = 152.0
FUSED_BLOCKS = 3
MASKED_LOGIT = -1e30
Z_CLAMP = 120.0

_f32 = jnp.float32
_bf16 = jnp.bfloat16


def _dot(a, b):
    return jnp.dot(a, b, preferred_element_type=_f32)


def _rms(x, gain):
    return x * lax.rsqrt(jnp.mean(x * x, axis=-1, keepdims=True) + EPS) * gain


def _const_spec(shape):
    return pl.BlockSpec(shape, lambda *_: (0,) * len(shape), pipeline_mode=pl.Buffered(1))


def _ada_kernel(ct_ref, w_ref, b_ref, o_ref):
    w = w_ref[0]
    rows = [jnp.sum(w * ct_ref[:, i:i + 1], axis=0, keepdims=True) for i in range(ct_ref.shape[1])]
    o_ref[0] = jnp.concatenate(rows, axis=0) + b_ref[0, 0]


def _ada(c, w_ada, b_ada):
    depth, d, _ = w_ada.shape
    b = c.shape[0]
    return pl.pallas_call(
        _ada_kernel,
        grid=(depth, N_MOD),
        in_specs=[
            pl.BlockSpec((d, b), lambda l, j: (0, 0)),
            pl.BlockSpec((1, d, d), lambda l, j: (l, 0, j)),
            pl.BlockSpec((1, 1, 1, d), lambda l, j: (l, j, 0, 0)),
        ],
        out_specs=pl.BlockSpec((1, b, d), lambda l, j: (l, 0, j)),
        out_shape=jax.ShapeDtypeStruct((depth, b, N_MOD * d), _f32),
        compiler_params=pltpu.CompilerParams(
            dimension_semantics=("arbitrary", "arbitrary"), vmem_limit_bytes=VMEM_LIMIT),
        name="ada",
    )(c.T, w_ada, b_ada.reshape(depth, N_MOD, 1, d))


def _mixer_in_kernel(x_ref, mod_ref, g_ref, w_in_ref, cw_ref, wpc_ref,
                     q_ref, k_ref, v_ref, gc_ref, sgb_ref, carry_ref):
    d = x_ref.shape[-1]
    tm = x_ref.shape[1]
    cw = CONV_WIDTH

    @pl.when(pl.program_id(1) == 0)
    def _():
        carry_ref[...] = jnp.zeros_like(carry_ref)

    shift = mod_ref[0, :, 0:d]
    scale = mod_ref[0, :, d:2 * d]
    x = x_ref[0]
    h = (_rms(x, g_ref[...]) * (1.0 + scale) + shift).astype(_bf16)

    bg = _dot(h, w_in_ref[:, 0:cw])
    cg = _dot(h, w_in_ref[:, cw:2 * cw])
    u = _dot(h, w_in_ref[:, 2 * cw:3 * cw])

    o = 3 * cw
    aw = ATTN_WIDTH
    q_ref[0] = (_dot(h, w_in_ref[:, o:o + aw]) * (LOG2E / math.sqrt(HEAD_DIM))).astype(_bf16)
    k_ref[0] = _dot(h, w_in_ref[:, o + aw:o + 2 * aw]).astype(_bf16)
    v_ref[0] = _dot(h, w_in_ref[:, o + 2 * aw:o + 3 * aw]).astype(_bf16)
    o += 3 * aw
    gb = _dot(h, w_in_ref[:, o + d:o + 2 * d])
    sgb_ref[0] = jax.nn.sigmoid(gb)
    ga = _dot(h, w_in_ref[:, o:o + d])

    vv = cg * u
    row = lax.broadcasted_iota(jnp.int32, vv.shape, 0)
    prev1 = carry_ref[1:2, :]
    prev2 = carry_ref[0:1, :]
    v1 = jnp.where(row == 0, prev1, pltpu.roll(vv, 1, axis=0))
    v2 = jnp.where(row == 0, prev2, jnp.where(row == 1, prev1, pltpu.roll(vv, 2, axis=0)))
    carry_ref[...] = vv[tm - 2:tm, :]
    conv = cw_ref[0:1, :] * v2 + cw_ref[1:2, :] * v1 + cw_ref[2:3, :] * vv
    y_conv = _dot((bg * conv).astype(_bf16), wpc_ref[...])
    gc_ref[0] = jax.nn.sigmoid(ga) * y_conv


def _mixer_in(x, mod, g_pre, w_in, conv_w, w_proj_conv):
    b, s, d = x.shape
    tm = SEQ_TILE
    tile = lambda width: pl.BlockSpec((1, tm, width), lambda i, j: (i, j, 0))
    return pl.pallas_call(
        _mixer_in_kernel,
        grid=(b, s // tm),
        in_specs=[
            tile(d),
            pl.BlockSpec((1, 1, N_MOD * d), lambda i, j: (i, 0, 0)),
            _const_spec((1, d)),
            _const_spec(w_in.shape),
            _const_spec(conv_w.shape),
            _const_spec(w_proj_conv.shape),
        ],
        out_specs=[tile(ATTN_WIDTH), tile(ATTN_WIDTH), tile(ATTN_WIDTH), tile(d), tile(d)],
        out_shape=[
            jax.ShapeDtypeStruct((b, s, ATTN_WIDTH), _bf16),
            jax.ShapeDtypeStruct((b, s, ATTN_WIDTH), _bf16),
            jax.ShapeDtypeStruct((b, s, ATTN_WIDTH), _bf16),
            jax.ShapeDtypeStruct((b, s, d), _f32),
            jax.ShapeDtypeStruct((b, s, d), _f32),
        ],
        scratch_shapes=[pltpu.VMEM((CONV_TAPS - 1, CONV_WIDTH), _f32)],
        compiler_params=pltpu.CompilerParams(
            dimension_semantics=("arbitrary", "arbitrary"), vmem_limit_bytes=VMEM_LIMIT),
        name="mixer_in",
    )(x, mod.reshape(b, 1, N_MOD * d), g_pre.reshape(1, d), w_in, conv_w, w_proj_conv)


def _attn_kernel(q_ref, k_ref, v_ref, o_ref, qh_ref, acc_ref, run_ref):
    tq = q_ref.shape[1]
    qi = pl.program_id(2)
    n_heads = qh_ref.shape[0]
    lanes_of = lambda h: slice((h // HEADS_PER_TILE) * LANES, (h // HEADS_PER_TILE + 1) * LANES)

    lane = lax.broadcasted_iota(jnp.int32, (tq, LANES), 1)
    own = [lane < HEAD_DIM, lane >= HEAD_DIM]
    for h in range(n_heads):
        q = q_ref[0, :, lanes_of(h)]
        qh_ref[h] = jnp.where(own[h % HEADS_PER_TILE], q, jnp.zeros_like(q))
    acc_ref[...] = jnp.zeros_like(acc_ref)
    run_ref[...] = jnp.zeros_like(run_ref)

    tri2 = (lax.broadcasted_iota(jnp.int32, (K_TILE, K_TILE), 0)
            >= lax.broadcasted_iota(jnp.int32, (K_TILE, K_TILE), 1)).astype(_bf16)

    heads = range(n_heads)

    def rows_of(kb):
        return pl.ds(pl.multiple_of(kb * K_TILE, K_TILE), K_TILE)

    def logits(kb, h):
        return lax.dot_general(qh_ref[h], k_ref[0, rows_of(kb), lanes_of(h)],
                               (((1,), (1,)), ((), ())), preferred_element_type=_f32)

    def suffix_sums(z, diagonal):
        sp = jnp.maximum(jnp.log2(1.0 + jnp.exp2(jnp.minimum(z, Z_CLAMP))), z)
        if diagonal:
            keep = (lax.broadcasted_iota(jnp.int32, z.shape, 1)
                    < lax.broadcasted_iota(jnp.int32, z.shape, 0))
            sp = jnp.where(keep, sp, 0.0)
            z = jnp.where(keep, z, MASKED_LOGIT)
        return z, _dot(sp.astype(_bf16), tri2)

    def accumulate(kb, h, z, suffix):
        run = run_ref[h]
        a = jnp.exp2(z - suffix - jnp.concatenate([run] * (K_TILE // LANES), axis=1))
        acc_ref[h] += _dot(a.astype(_bf16), v_ref[0, rows_of(kb), lanes_of(h)])
        run_ref[h] = run + jnp.broadcast_to(suffix[:, 0:1], run.shape)

    def sweep(blocks):
        n = len(blocks)
        z = [None] * n
        suffix = [None] * n
        for phase in range(n + 2):
            for h in heads:
                if 0 <= phase - 2 < n:
                    accumulate(blocks[phase - 2], h, z[phase - 2][h], suffix[phase - 2][h])
                if 0 <= phase - 1 < n:
                    z[phase - 1][h], s = suffix_sums(z[phase - 1][h], diagonal=(phase == 1))
                    suffix[phase - 1] = (suffix[phase - 1] or []) + [s]
                if phase < n:
                    z[phase] = (z[phase] or []) + [logits(blocks[phase], h)]

    @pl.when(qi >= FUSED_BLOCKS - 1)
    def _():
        sweep([qi - i for i in range(FUSED_BLOCKS)])

    @pl.when(qi < FUSED_BLOCKS - 1)
    def _():
        sweep([qi])

    def more(carry):
        kb, run_min = carry
        return jnp.logical_and(kb >= 0, run_min < UNDERFLOW_LOG2)

    def body(carry):
        kb, _ = carry
        z = [logits(kb, h) for h in heads]
        suffix = [suffix_sums(z[h], diagonal=False)[1] for h in heads]
        for h in heads:
            accumulate(kb, h, z[h], suffix[h])
        return kb - 1, jnp.min(run_ref[...])

    first_left = jnp.where(qi >= FUSED_BLOCKS - 1, qi - FUSED_BLOCKS, qi - 1)
    lax.while_loop(more, body, (first_left, jnp.min(run_ref[...])))

    for h in range(0, n_heads, HEADS_PER_TILE):
        o_ref[0, :, lanes_of(h)] = jnp.where(own[0], acc_ref[h], acc_ref[h + 1]).astype(o_ref.dtype)


def _attention(q, k, v):
    b, s, w = q.shape
    width = ATTN_LANE_TILES * LANES
    n_heads = ATTN_LANE_TILES * HEADS_PER_TILE
    assert Q_TILE == K_TILE and s % Q_TILE == 0 and w % width == 0
    seq = pl.BlockSpec((1, s, width), lambda i, h, j: (i, 0, h))
    tile = pl.BlockSpec((1, Q_TILE, width), lambda i, h, j: (i, j, h))
    return pl.pallas_call(
        _attn_kernel,
        grid=(b, w // width, s // Q_TILE),
        in_specs=[tile, seq, seq],
        out_specs=tile,
        out_shape=jax.ShapeDtypeStruct((b, s, w), _bf16),
        scratch_shapes=[
            pltpu.VMEM((n_heads, Q_TILE, LANES), _bf16),
            pltpu.VMEM((n_heads, Q_TILE, LANES), _f32),
            pltpu.VMEM((n_heads, Q_TILE, LANES), _f32),
        ],
        compiler_params=pltpu.CompilerParams(
            dimension_semantics=("arbitrary", "arbitrary", "arbitrary"),
            vmem_limit_bytes=VMEM_LIMIT),
        name="attention",
    )(q, k, v)


def _mixer_out_kernel(x_ref, o_ref, gc_ref, sgb_ref, mod_ref, gains_ref,
                      wpa_ref, wout_ref, w1_ref, w2_ref, out_ref):
    d = x_ref.shape[-1]
    mod = lambda i: mod_ref[0, :, i * d:(i + 1) * d]
    gate1, shift2, scale2, gate2 = mod(2), mod(3), mod(4), mod(5)
    g_post_mix = gains_ref[0:1, :]
    g_pre_mlp = gains_ref[1:2, :]
    g_post_mlp = gains_ref[2:3, :]

    tm = x_ref.shape[1]
    subs = [pl.ds(r, tm // ROW_SUBTILES) for r in range(0, tm, tm // ROW_SUBTILES)]

    mix = []
    for rows in subs:
        y_attn = _dot(o_ref[0, rows, :], wpa_ref[...])
        merged = gc_ref[0, rows, :] + sgb_ref[0, rows, :] * y_attn
        mix.append(_dot(merged.astype(_bf16), wout_ref[...]))
    x1, h2 = [], []
    for i, rows in enumerate(subs):
        x1.append(x_ref[0, rows, :] + gate1 * _rms(mix[i], g_post_mix))
        h2.append((_rms(x1[i], g_pre_mlp) * (1.0 + scale2) + shift2).astype(_bf16))
    for i, rows in enumerate(subs):
        ff = jnp.zeros_like(x1[i])
        for j in range(0, w1_ref.shape[1], FF_CHUNK):
            hid = jnp.maximum(_dot(h2[i], w1_ref[:, j:j + FF_CHUNK]), 0.0)
            ff = ff + _dot((hid * hid).astype(_bf16), w2_ref[j:j + FF_CHUNK, :])
        out_ref[0, rows, :] = x1[i] + gate2 * _rms(ff, g_post_mlp)


def _mixer_out(x, o, gc, sgb, mod, gains, w_proj_attn, w_out, w1, w2):
    b, s, d = x.shape
    tm = SEQ_TILE
    tile = lambda width: pl.BlockSpec((1, tm, width), lambda i, j: (i, j, 0))
    return pl.pallas_call(
        _mixer_out_kernel,
        grid=(b, s // tm),
        in_specs=[
            tile(d), tile(ATTN_WIDTH), tile(d), tile(d),
            pl.BlockSpec((1, 1, N_MOD * d), lambda i, j: (i, 0, 0)),
            _const_spec(gains.shape),
            _const_spec(w_proj_attn.shape),
            _const_spec(w_out.shape),
            _const_spec(w1.shape),
            _const_spec(w2.shape),
        ],
        out_specs=tile(d),
        out_shape=jax.ShapeDtypeStruct((b, s, d), _f32),
        compiler_params=pltpu.CompilerParams(
            dimension_semantics=("arbitrary", "arbitrary"), vmem_limit_bytes=VMEM_LIMIT),
        name="mixer_out",
    )(x, o, gc, sgb, mod.reshape(b, 1, N_MOD * d), gains, w_proj_attn, w_out, w1, w2)


def kernel(x, c, w_ada, b_ada, g_pre_mix, g_post_mix, g_pre_mlp, g_post_mlp, w_in, conv_w,
           w_proj_conv, w_proj_attn, w_out, w_mlp_in, w_mlp_out):
    depth = w_ada.shape[0]
    mod = _ada(c, w_ada, b_ada)
    for l in range(depth):
        q, k, v, gc, sgb = _mixer_in(x, mod[l], g_pre_mix[l], w_in[l].astype(_bf16), conv_w[l],
                                     w_proj_conv[l].astype(_bf16))
        o = _attention(q, k, v)
        gains = jnp.stack([g_post_mix[l], g_pre_mlp[l], g_post_mlp[l]])
        x = _mixer_out(x, o, gc, sgb, mod[l], gains, w_proj_attn[l].astype(_bf16),
                       w_out[l].astype(_bf16), w_mlp_in[l].astype(_bf16),
                       w_mlp_out[l].astype(_bf16))
    return x
```

```python
import functools
import math

import jax
import jax.numpy as jnp
from jax import lax
from jax.experimental import pallas as pl
from jax.experimental.pallas import tpu as pltpu

CONV_WIDTH = 512
CONV_TAPS = 3
ATTN_WIDTH = 512
HEAD_DIM = 64
N_MOD = 6
EPS = 1e-6

LANES = 128
HEADS_PER_TILE = LANES // HEAD_DIM
ATTN_LANE_TILES = 4
SEQ_TILE = 512
Q_TILE = 256
K_TILE = 256
ADA_COLS = 2048
FF_CHUNK = 1024
ROW_SUBTILES = 2
VMEM_LIMIT = 56 * 1024 * 1024
LOG2E = math.log2(math.e)
UNDERFLOW_LOG2 = 152.0
FUSED_BLOCKS = 3
MASKED_LOGIT = -1e30
Z_CLAMP = 120.0

_f32 = jnp.float32
_bf16 = jnp.bfloat16


def _dot(a, b):
    return jnp.dot(a, b, preferred_element_type=_f32)


def _rms(x, gain):
    return x * lax.rsqrt(jnp.mean(x * x, axis=-1, keepdims=True) + EPS) * gain


def _const_spec(shape):
    return pl.BlockSpec(shape, lambda *_: (0,) * len(shape), pipeline_mode=pl.Buffered(1))


def _ada_kernel(ct_ref, w_ref, b_ref, o_ref):
    w = w_ref[0]
    rows = [jnp.sum(w * ct_ref[:, i:i + 1], axis=0, keepdims=True) for i in range(ct_ref.shape[1])]
    o_ref[0] = jnp.concatenate(rows, axis=0) + b_ref[0, 0]


def _ada(c, w_ada, b_ada):
    depth, d, width = w_ada.shape
    b = c.shape[0]
    cols = ADA_COLS
    return pl.pallas_call(
        _ada_kernel,
        grid=(depth, width // cols),
        in_specs=[
            pl.BlockSpec((d, b), lambda l, j: (0, 0)),
            pl.BlockSpec((1, d, cols), lambda l, j: (l, 0, j)),
            pl.BlockSpec((1, 1, 1, cols), lambda l, j: (l, j, 0, 0)),
        ],
        out_specs=pl.BlockSpec((1, b, cols), lambda l, j: (l, 0, j)),
        out_shape=jax.ShapeDtypeStruct((depth, b, width), _f32),
        compiler_params=pltpu.CompilerParams(
            dimension_semantics=("arbitrary", "arbitrary"), vmem_limit_bytes=VMEM_LIMIT),
        name="ada",
    )(c.T, w_ada, b_ada.reshape(depth, width // cols, 1, cols))


def _mixer_in_kernel(x_ref, mod_ref, g_ref, w_in_ref, cw_ref, wpc_ref,
                     q_ref, k_ref, v_ref, gc_ref, sgb_ref, carry_ref):
    d = x_ref.shape[-1]
    tm = x_ref.shape[1]
    cw = CONV_WIDTH

    @pl.when(pl.program_id(1) == 0)
    def _():
        carry_ref[...] = jnp.zeros_like(carry_ref)

    shift = mod_ref[0, :, 0:d]
    scale = mod_ref[0, :, d:2 * d]
    x = x_ref[0]
    h = _rms(x, g_ref[...]) * (1.0 + scale) + shift

    bg = _dot(h, w_in_ref[:, 0:cw])
    cg = _dot(h, w_in_ref[:, cw:2 * cw])
    u = _dot(h, w_in_ref[:, 2 * cw:3 * cw])

    o = 3 * cw
    aw = ATTN_WIDTH
    q_ref[0] = (_dot(h, w_in_ref[:, o:o + aw]) * (LOG2E / math.sqrt(HEAD_DIM))).astype(_bf16)
    k_ref[0] = _dot(h, w_in_ref[:, o + aw:o + 2 * aw]).astype(_bf16)
    v_ref[0] = _dot(h, w_in_ref[:, o + 2 * aw:o + 3 * aw]).astype(_bf16)
    o += 3 * aw
    gb = _dot(h, w_in_ref[:, o + d:o + 2 * d])
    sgb_ref[0] = jax.nn.sigmoid(gb)
    ga = _dot(h, w_in_ref[:, o:o + d])

    vv = cg * u
    row = lax.broadcasted_iota(jnp.int32, vv.shape, 0)
    prev1 = carry_ref[1:2, :]
    prev2 = carry_ref[0:1, :]
    v1 = jnp.where(row == 0, prev1, pltpu.roll(vv, 1, axis=0))
    v2 = jnp.where(row == 0, prev2, jnp.where(row == 1, prev1, pltpu.roll(vv, 2, axis=0)))
    carry_ref[...] = vv[tm - 2:tm, :]
    conv = cw_ref[0:1, :] * v2 + cw_ref[1:2, :] * v1 + cw_ref[2:3, :] * vv
    y_conv = _dot(bg * conv, wpc_ref[...])
    gc_ref[0] = jax.nn.sigmoid(ga) * y_conv


def _mixer_in(x, mod, g_pre, w_in, conv_w, w_proj_conv):
    b, s, d = x.shape
    tm = SEQ_TILE
    tile = lambda width: pl.BlockSpec((1, tm, width), lambda i, j: (i, j, 0))
    return pl.pallas_call(
        _mixer_in_kernel,
        grid=(b, s // tm),
        in_specs=[
            tile(d),
            pl.BlockSpec((1, 1, N_MOD * d), lambda i, j: (i, 0, 0)),
            _const_spec((1, d)),
            _const_spec(w_in.shape),
            _const_spec(conv_w.shape),
            _const_spec(w_proj_conv.shape),
        ],
        out_specs=[tile(ATTN_WIDTH), tile(ATTN_WIDTH), tile(ATTN_WIDTH), tile(d), tile(d)],
        out_shape=[
            jax.ShapeDtypeStruct((b, s, ATTN_WIDTH), _bf16),
            jax.ShapeDtypeStruct((b, s, ATTN_WIDTH), _bf16),
            jax.ShapeDtypeStruct((b, s, ATTN_WIDTH), _bf16),
            jax.ShapeDtypeStruct((b, s, d), _f32),
            jax.ShapeDtypeStruct((b, s, d), _f32),
        ],
        scratch_shapes=[pltpu.VMEM((CONV_TAPS - 1, CONV_WIDTH), _f32)],
        compiler_params=pltpu.CompilerParams(
            dimension_semantics=("arbitrary", "arbitrary"), vmem_limit_bytes=VMEM_LIMIT),
        name="mixer_in",
    )(x, mod.reshape(b, 1, N_MOD * d), g_pre.reshape(1, d), w_in, conv_w, w_proj_conv)


def _attn_kernel(q_ref, k_ref, v_ref, o_ref, qh_ref, acc_ref, run_ref):
    tq = q_ref.shape[1]
    qi = pl.program_id(2)
    n_heads = qh_ref.shape[0]
    lanes_of = lambda h: slice((h // HEADS_PER_TILE) * LANES, (h // HEADS_PER_TILE + 1) * LANES)

    lane = lax.broadcasted_iota(jnp.int32, (tq, LANES), 1)
    own = [lane < HEAD_DIM, lane >= HEAD_DIM]
    for h in range(n_heads):
        q = q_ref[0, :, lanes_of(h)]
        qh_ref[h] = jnp.where(own[h % HEADS_PER_TILE], q, jnp.zeros_like(q))
    acc_ref[...] = jnp.zeros_like(acc_ref)
    run_ref[...] = jnp.zeros_like(run_ref)

    tri2 = (lax.broadcasted_iota(jnp.int32, (K_TILE, K_TILE), 0)
            >= lax.broadcasted_iota(jnp.int32, (K_TILE, K_TILE), 1)).astype(_bf16)

    heads = range(n_heads)

    def rows_of(kb):
        return pl.ds(pl.multiple_of(kb * K_TILE, K_TILE), K_TILE)

    def logits(kb, h):
        return lax.dot_general(qh_ref[h], k_ref[0, rows_of(kb), lanes_of(h)],
                               (((1,), (1,)), ((), ())), preferred_element_type=_f32)

    def suffix_sums(z, diagonal):
        sp = jnp.maximum(jnp.log2(1.0 + jnp.exp2(jnp.minimum(z, Z_CLAMP))), z)
        if diagonal:
            keep = (lax.broadcasted_iota(jnp.int32, z.shape, 1)
                    < lax.broadcasted_iota(jnp.int32, z.shape, 0))
            sp = jnp.where(keep, sp, 0.0)
            z = jnp.where(keep, z, MASKED_LOGIT)
        return z, _dot(sp.astype(_bf16), tri2)

    def accumulate(kb, h, z, suffix):
        run = run_ref[h]
        a = jnp.exp2(z - suffix - jnp.concatenate([run] * (K_TILE // LANES), axis=1))
        acc_ref[h] += _dot(a.astype(_bf16), v_ref[0, rows_of(kb), lanes_of(h)])
        run_ref[h] = run + jnp.broadcast_to(suffix[:, 0:1], run.shape)

    def sweep(blocks):
        n = len(blocks)
        z = [None] * n
        suffix = [None] * n
        for phase in range(n + 2):
            for h in heads:
                if 0 <= phase - 2 < n:
                    accumulate(blocks[phase - 2], h, z[phase - 2][h], suffix[phase - 2][h])
                if 0 <= phase - 1 < n:
                    z[phase - 1][h], s = suffix_sums(z[phase - 1][h], diagonal=(phase == 1))
                    suffix[phase - 1] = (suffix[phase - 1] or []) + [s]
                if phase < n:
                    z[phase] = (z[phase] or []) + [logits(blocks[phase], h)]

    @pl.when(qi >= FUSED_BLOCKS - 1)
    def _():
        sweep([qi - i for i in range(FUSED_BLOCKS)])

    @pl.when(qi < FUSED_BLOCKS - 1)
    def _():
        sweep([qi])

    def more(carry):
        kb, run_min = carry
        return jnp.logical_and(kb >= 0, run_min < UNDERFLOW_LOG2)

    def body(carry):
        kb, _ = carry
        z = [logits(kb, h) for h in heads]
        suffix = [suffix_sums(z[h], diagonal=False)[1] for h in heads]
        for h in heads:
            accumulate(kb, h, z[h], suffix[h])
        return kb - 1, jnp.min(run_ref[...])

    first_left = jnp.where(qi >= FUSED_BLOCKS - 1, qi - FUSED_BLOCKS, qi - 1)
    lax.while_loop(more, body, (first_left, jnp.min(run_ref[...])))

    for h in range(0, n_heads, HEADS_PER_TILE):
        o_ref[0, :, lanes_of(h)] = jnp.where(own[0], acc_ref[h], acc_ref[h + 1]).astype(o_ref.dtype)


def _attention(q, k, v):
    b, s, w = q.shape
    width = ATTN_LANE_TILES * LANES
    n_heads = ATTN_LANE_TILES * HEADS_PER_TILE
    assert Q_TILE == K_TILE and s % Q_TILE == 0 and w % width == 0
    seq = pl.BlockSpec((1, s, width), lambda i, h, j: (i, 0, h))
    tile = pl.BlockSpec((1, Q_TILE, width), lambda i, h, j: (i, j, h))
    return pl.pallas_call(
        _attn_kernel,
        grid=(b, w // width, s // Q_TILE),
        in_specs=[tile, seq, seq],
        out_specs=tile,
        out_shape=jax.ShapeDtypeStruct((b, s, w), _bf16),
        scratch_shapes=[
            pltpu.VMEM((n_heads, Q_TILE, LANES), _bf16),
            pltpu.VMEM((n_heads, Q_TILE, LANES), _f32),
            pltpu.VMEM((n_heads, Q_TILE, LANES), _f32),
        ],
        compiler_params=pltpu.CompilerParams(
            dimension_semantics=("arbitrary", "arbitrary", "arbitrary"),
            vmem_limit_bytes=VMEM_LIMIT),
        name="attention",
    )(q, k, v)


def _mixer_out_kernel(x_ref, o_ref, gc_ref, sgb_ref, mod_ref, gains_ref,
                      wpa_ref, wout_ref, w1_ref, w2_ref, out_ref):
    d = x_ref.shape[-1]
    mod = lambda i: mod_ref[0, :, i * d:(i + 1) * d]
    gate1, shift2, scale2, gate2 = mod(2), mod(3), mod(4), mod(5)
    g_post_mix = gains_ref[0:1, :]
    g_pre_mlp = gains_ref[1:2, :]
    g_post_mlp = gains_ref[2:3, :]

    tm = x_ref.shape[1]
    subs = [pl.ds(r, tm // ROW_SUBTILES) for r in range(0, tm, tm // ROW_SUBTILES)]

    mix = []
    for rows in subs:
        y_attn = _dot(o_ref[0, rows, :], wpa_ref[...])
        merged = gc_ref[0, rows, :] + sgb_ref[0, rows, :] * y_attn
        mix.append(_dot(merged, wout_ref[...]))
    x1, h2 = [], []
    for i, rows in enumerate(subs):
        x1.append(x_ref[0, rows, :] + gate1 * _rms(mix[i], g_post_mix))
        h2.append(_rms(x1[i], g_pre_mlp) * (1.0 + scale2) + shift2)
    for i, rows in enumerate(subs):
        ff = jnp.zeros_like(x1[i])
        for j in range(0, w1_ref.shape[1], FF_CHUNK):
            hid = jnp.maximum(_dot(h2[i], w1_ref[:, j:j + FF_CHUNK]), 0.0)
            ff = ff + _dot((hid * hid).astype(_bf16), w2_ref[j:j + FF_CHUNK, :])
        out_ref[0, rows, :] = x1[i] + gate2 * _rms(ff, g_post_mlp)


def _mixer_out(x, o, gc, sgb, mod, gains, w_proj_attn, w_out, w1, w2):
    b, s, d = x.shape
    tm = SEQ_TILE
    tile = lambda width: pl.BlockSpec((1, tm, width), lambda i, j: (i, j, 0))
    return pl.pallas_call(
        _mixer_out_kernel,
        grid=(b, s // tm),
        in_specs=[
            tile(d), tile(ATTN_WIDTH), tile(d), tile(d),
            pl.BlockSpec((1, 1, N_MOD * d), lambda i, j: (i, 0, 0)),
            _const_spec(gains.shape),
            _const_spec(w_proj_attn.shape),
            _const_spec(w_out.shape),
            _const_spec(w1.shape),
            _const_spec(w2.shape),
        ],
        out_specs=tile(d),
        out_shape=jax.ShapeDtypeStruct((b, s, d), _f32),
        compiler_params=pltpu.CompilerParams(
            dimension_semantics=("arbitrary", "arbitrary"), vmem_limit_bytes=VMEM_LIMIT),
        name="mixer_out",
    )(x, o, gc, sgb, mod.reshape(b, 1, N_MOD * d), gains, w_proj_attn, w_out, w1, w2)


def kernel(x, c, w_ada, b_ada, g_pre_mix, g_post_mix, g_pre_mlp, g_post_mlp, w_in, conv_w,
           w_proj_conv, w_proj_attn, w_out, w_mlp_in, w_mlp_out):
    depth = w_ada.shape[0]
    mod = _ada(c, w_ada, b_ada)
    for l in range(depth):
        q, k, v, gc, sgb = _mixer_in(x, mod[l], g_pre_mix[l], w_in[l], conv_w[l], w_proj_conv[l])
        o = _attention(q, k, v)
        gains = jnp.stack([g_post_mix[l], g_pre_mlp[l], g_post_mlp[l]])
        x = _mixer_out(x, o, gc, sgb, mod[l], gains, w_proj_attn[l], w_out[l],
                       w_mlp_in[l], w_mlp_out[l].astype(_bf16))
    return x
```

```python
import functools
import math

import jax
import jax.numpy as jnp
from jax import lax
from jax.experimental import pallas as pl
from jax.experimental.pallas import tpu as pltpu

CONV_WIDTH = 512
CONV_TAPS = 3
ATTN_WIDTH = 512
HEAD_DIM = 64
N_MOD = 6
EPS = 1e-6

LANES = 128
HEADS_PER_TILE = LANES // HEAD_DIM
ATTN_LANE_TILES = 4
SEQ_TILE = 512
Q_TILE = 256
K_TILE = 256
ADA_COLS = 2048
FF_CHUNK = 1024
ROW_SUBTILES = 2
VMEM_LIMIT = 56 * 1024 * 1024
LOG2E = math.log2(math.e)
UNDERFLOW_LOG2 = 152.0
FUSED_BLOCKS = 3
MASKED_LOGIT = -1e30
Z_CLAMP = 120.0

_f32 = jnp.float32
_bf16 = jnp.bfloat16


def _dot(a, b):
    return jnp.dot(a, b, preferred_element_type=_f32)


def _rms(x, gain):
    return x * lax.rsqrt(jnp.mean(x * x, axis=-1, keepdims=True) + EPS) * gain


def _layer_spec(stacked, layer):
    rest = stacked.shape[1:]
    return pl.BlockSpec((None,) + rest, lambda *_: (layer,) + (0,) * len(rest),
                        pipeline_mode=pl.Buffered(1))


def _const_spec(shape):
    return pl.BlockSpec(shape, lambda *_: (0,) * len(shape), pipeline_mode=pl.Buffered(1))


def _ada_kernel(ct_ref, w_ref, b_ref, o_ref):
    w = w_ref[0]
    rows = [jnp.sum(w * ct_ref[:, i:i + 1], axis=0, keepdims=True) for i in range(ct_ref.shape[1])]
    o_ref[0] = jnp.concatenate(rows, axis=0) + b_ref[0, 0]


def _ada(c, w_ada, b_ada):
    depth, d, width = w_ada.shape
    b = c.shape[0]
    cols = ADA_COLS
    return pl.pallas_call(
        _ada_kernel,
        grid=(depth, width // cols),
        in_specs=[
            pl.BlockSpec((d, b), lambda l, j: (0, 0)),
            pl.BlockSpec((1, d, cols), lambda l, j: (l, 0, j)),
            pl.BlockSpec((1, 1, 1, cols), lambda l, j: (l, j, 0, 0)),
        ],
        out_specs=pl.BlockSpec((1, b, cols), lambda l, j: (l, 0, j)),
        out_shape=jax.ShapeDtypeStruct((depth, b, width), _f32),
        compiler_params=pltpu.CompilerParams(
            dimension_semantics=("arbitrary", "arbitrary"), vmem_limit_bytes=VMEM_LIMIT),
        name="ada",
    )(c.T, w_ada, b_ada.reshape(depth, width // cols, 1, cols))


def _mixer_in_kernel(x_ref, mod_ref, g_ref, w_in_ref, cw_ref, wpc_ref,
                     q_ref, k_ref, v_ref, gc_ref, sgb_ref, carry_ref):
    d = x_ref.shape[-1]
    tm = x_ref.shape[1]
    cw = CONV_WIDTH

    @pl.when(pl.program_id(1) == 0)
    def _():
        carry_ref[...] = jnp.zeros_like(carry_ref)

    shift = mod_ref[0, :, 0:d]
    scale = mod_ref[0, :, d:2 * d]
    h = _rms(x_ref[0], g_ref[...]) * (1.0 + scale) + shift

    bg = _dot(h, w_in_ref[:, 0:cw])
    cg = _dot(h, w_in_ref[:, cw:2 * cw])
    u = _dot(h, w_in_ref[:, 2 * cw:3 * cw])

    o = 3 * cw
    aw = ATTN_WIDTH
    q_ref[0] = (_dot(h, w_in_ref[:, o:o + aw]) * (LOG2E / math.sqrt(HEAD_DIM))).astype(_bf16)
    k_ref[0] = _dot(h, w_in_ref[:, o + aw:o + 2 * aw]).astype(_bf16)
    v_ref[0] = _dot(h, w_in_ref[:, o + 2 * aw:o + 3 * aw]).astype(_bf16)
    o += 3 * aw
    gb = _dot(h, w_in_ref[:, o + d:o + 2 * d])
    sgb_ref[0] = jax.nn.sigmoid(gb)
    ga = _dot(h, w_in_ref[:, o:o + d])

    vv = cg * u
    row = lax.broadcasted_iota(jnp.int32, vv.shape, 0)
    prev1 = carry_ref[1:2, :]
    prev2 = carry_ref[0:1, :]
    v1 = jnp.where(row == 0, prev1, pltpu.roll(vv, 1, axis=0))
    v2 = jnp.where(row == 0, prev2, jnp.where(row == 1, prev1, pltpu.roll(vv, 2, axis=0)))
    carry_ref[...] = vv[tm - 2:tm, :]
    conv = cw_ref[0:1, :] * v2 + cw_ref[1:2, :] * v1 + cw_ref[2:3, :] * vv
    y_conv = _dot(bg * conv, wpc_ref[...])
    gc_ref[0] = jax.nn.sigmoid(ga) * y_conv


def _mixer_in(layer, x, mod, g_pre, w_in, conv_w, w_proj_conv):
    b, s, d = x.shape
    tm = SEQ_TILE
    tile = lambda width: pl.BlockSpec((1, tm, width), lambda i, j: (i, j, 0))
    return pl.pallas_call(
        _mixer_in_kernel,
        grid=(b, s // tm),
        in_specs=[
            tile(d),
            pl.BlockSpec((1, 1, N_MOD * d), lambda i, j: (i, 0, 0)),
            _const_spec((1, d)),
            _layer_spec(w_in, layer),
            _layer_spec(conv_w, layer),
            _layer_spec(w_proj_conv, layer),
        ],
        out_specs=[tile(ATTN_WIDTH), tile(ATTN_WIDTH), tile(ATTN_WIDTH), tile(d), tile(d)],
        out_shape=[
            jax.ShapeDtypeStruct((b, s, ATTN_WIDTH), _bf16),
            jax.ShapeDtypeStruct((b, s, ATTN_WIDTH), _bf16),
            jax.ShapeDtypeStruct((b, s, ATTN_WIDTH), _bf16),
            jax.ShapeDtypeStruct((b, s, d), _f32),
            jax.ShapeDtypeStruct((b, s, d), _f32),
        ],
        scratch_shapes=[pltpu.VMEM((CONV_TAPS - 1, CONV_WIDTH), _f32)],
        compiler_params=pltpu.CompilerParams(
            dimension_semantics=("arbitrary", "arbitrary"), vmem_limit_bytes=VMEM_LIMIT),
        name="mixer_in",
    )(x, mod.reshape(b, 1, N_MOD * d), g_pre.reshape(1, d), w_in, conv_w, w_proj_conv)


def _attn_kernel(q_ref, k_ref, v_ref, o_ref, qh_ref, acc_ref, run_ref):
    tq = q_ref.shape[1]
    qi = pl.program_id(2)
    n_heads = qh_ref.shape[0]
    lanes_of = lambda h: slice((h // HEADS_PER_TILE) * LANES, (h // HEADS_PER_TILE + 1) * LANES)

    lane = lax.broadcasted_iota(jnp.int32, (tq, LANES), 1)
    own = [lane < HEAD_DIM, lane >= HEAD_DIM]
    for h in range(n_heads):
        q = q_ref[0, :, lanes_of(h)]
        qh_ref[h] = jnp.where(own[h % HEADS_PER_TILE], q, jnp.zeros_like(q))
    acc_ref[...] = jnp.zeros_like(acc_ref)
    run_ref[...] = jnp.zeros_like(run_ref)

    tri2 = (lax.broadcasted_iota(jnp.int32, (K_TILE, K_TILE), 0)
            >= lax.broadcasted_iota(jnp.int32, (K_TILE, K_TILE), 1)).astype(_bf16)

    heads = range(n_heads)

    def rows_of(kb):
        return pl.ds(pl.multiple_of(kb * K_TILE, K_TILE), K_TILE)

    def logits(kb, h):
        return lax.dot_general(qh_ref[h], k_ref[0, rows_of(kb), lanes_of(h)],
                               (((1,), (1,)), ((), ())), preferred_element_type=_f32)

    def suffix_sums(z, diagonal):
        sp = jnp.maximum(jnp.log2(1.0 + jnp.exp2(jnp.minimum(z, Z_CLAMP))), z)
        if diagonal:
            keep = (lax.broadcasted_iota(jnp.int32, z.shape, 1)
                    < lax.broadcasted_iota(jnp.int32, z.shape, 0))
            sp = jnp.where(keep, sp, 0.0)
            z = jnp.where(keep, z, MASKED_LOGIT)
        return z, _dot(sp.astype(_bf16), tri2)

    def accumulate(kb, h, z, suffix):
        run = run_ref[h]
        a = jnp.exp2(z - suffix - jnp.concatenate([run] * (K_TILE // LANES), axis=1))
        acc_ref[h] += _dot(a.astype(_bf16), v_ref[0, rows_of(kb), lanes_of(h)])
        run_ref[h] = run + jnp.broadcast_to(suffix[:, 0:1], run.shape)

    def sweep(blocks):
        n = len(blocks)
        z = [None] * n
        suffix = [None] * n
        for phase in range(n + 2):
            for h in heads:
                if 0 <= phase - 2 < n:
                    accumulate(blocks[phase - 2], h, z[phase - 2][h], suffix[phase - 2][h])
                if 0 <= phase - 1 < n:
                    z[phase - 1][h], s = suffix_sums(z[phase - 1][h], diagonal=(phase == 1))
                    suffix[phase - 1] = (suffix[phase - 1] or []) + [s]
                if phase < n:
                    z[phase] = (z[phase] or []) + [logits(blocks[phase], h)]

    @pl.when(qi >= FUSED_BLOCKS - 1)
    def _():
        sweep([qi - i for i in range(FUSED_BLOCKS)])

    @pl.when(qi < FUSED_BLOCKS - 1)
    def _():
        sweep([qi])

    def more(carry):
        kb, run_min = carry
        return jnp.logical_and(kb >= 0, run_min < UNDERFLOW_LOG2)

    def body(carry):
        kb, _ = carry
        z = [logits(kb, h) for h in heads]
        suffix = [suffix_sums(z[h], diagonal=False)[1] for h in heads]
        for h in heads:
            accumulate(kb, h, z[h], suffix[h])
        return kb - 1, jnp.min(run_ref[...])

    first_left = jnp.where(qi >= FUSED_BLOCKS - 1, qi - FUSED_BLOCKS, qi - 1)
    lax.while_loop(more, body, (first_left, jnp.min(run_ref[...])))

    for h in range(0, n_heads, HEADS_PER_TILE):
        o_ref[0, :, lanes_of(h)] = jnp.where(own[0], acc_ref[h], acc_ref[h + 1]).astype(o_ref.dtype)


def _attention(q, k, v):
    b, s, w = q.shape
    width = ATTN_LANE_TILES * LANES
    n_heads = ATTN_LANE_TILES * HEADS_PER_TILE
    assert Q_TILE == K_TILE and s % Q_TILE == 0 and w % width == 0
    seq = pl.BlockSpec((1, s, width), lambda i, h, j: (i, 0, h))
    tile = pl.BlockSpec((1, Q_TILE, width), lambda i, h, j: (i, j, h))
    return pl.pallas_call(
        _attn_kernel,
        grid=(b, w // width, s // Q_TILE),
        in_specs=[tile, seq, seq],
        out_specs=tile,
        out_shape=jax.ShapeDtypeStruct((b, s, w), _bf16),
        scratch_shapes=[
            pltpu.VMEM((n_heads, Q_TILE, LANES), _bf16),
            pltpu.VMEM((n_heads, Q_TILE, LANES), _f32),
            pltpu.VMEM((n_heads, Q_TILE, LANES), _f32),
        ],
        compiler_params=pltpu.CompilerParams(
            dimension_semantics=("arbitrary", "arbitrary", "arbitrary"),
            vmem_limit_bytes=VMEM_LIMIT),
        name="attention",
    )(q, k, v)


def _mixer_out_kernel(x_ref, o_ref, gc_ref, sgb_ref, mod_ref, gains_ref,
                      wpa_ref, wout_ref, w1_ref, w2_ref, out_ref):
    d = x_ref.shape[-1]
    mod = lambda i: mod_ref[0, :, i * d:(i + 1) * d]
    gate1, shift2, scale2, gate2 = mod(2), mod(3), mod(4), mod(5)
    g_post_mix = gains_ref[0:1, :]
    g_pre_mlp = gains_ref[1:2, :]
    g_post_mlp = gains_ref[2:3, :]

    tm = x_ref.shape[1]
    subs = [pl.ds(r, tm // ROW_SUBTILES) for r in range(0, tm, tm // ROW_SUBTILES)]

    mix = []
    for rows in subs:
        y_attn = _dot(o_ref[0, rows, :], wpa_ref[...])
        merged = gc_ref[0, rows, :] + sgb_ref[0, rows, :] * y_attn
        mix.append(_dot(merged, wout_ref[...]))
    x1, h2 = [], []
    for i, rows in enumerate(subs):
        x1.append(x_ref[0, rows, :] + gate1 * _rms(mix[i], g_post_mix))
        h2.append(_rms(x1[i], g_pre_mlp) * (1.0 + scale2) + shift2)
    for i, rows in enumerate(subs):
        ff = jnp.zeros_like(x1[i])
        for j in range(0, w1_ref.shape[1], FF_CHUNK):
            hid = jnp.maximum(_dot(h2[i], w1_ref[:, j:j + FF_CHUNK]), 0.0)
            ff = ff + _dot((hid * hid).astype(_bf16), w2_ref[j:j + FF_CHUNK, :])
        out_ref[0, rows, :] = x1[i] + gate2 * _rms(ff, g_post_mlp)


def _mixer_out(layer, x, o, gc, sgb, mod, gains, w_proj_attn, w_out, w1, w2):
    b, s, d = x.shape
    tm = SEQ_TILE
    tile = lambda width: pl.BlockSpec((1, tm, width), lambda i, j: (i, j, 0))
    return pl.pallas_call(
        _mixer_out_kernel,
        grid=(b, s // tm),
        in_specs=[
            tile(d), tile(ATTN_WIDTH), tile(d), tile(d),
            pl.BlockSpec((1, 1, N_MOD * d), lambda i, j: (i, 0, 0)),
            _const_spec(gains.shape),
            _layer_spec(w_proj_attn, layer),
            _layer_spec(w_out, layer),
            _layer_spec(w1, layer),
            _layer_spec(w2, layer),
        ],
        out_specs=tile(d),
        out_shape=jax.ShapeDtypeStruct((b, s, d), _f32),
        compiler_params=pltpu.CompilerParams(
            dimension_semantics=("arbitrary", "arbitrary"), vmem_limit_bytes=VMEM_LIMIT),
        name="mixer_out",
    )(x, o, gc, sgb, mod.reshape(b, 1, N_MOD * d), gains, w_proj_attn, w_out, w1, w2)


def kernel(x, c, w_ada, b_ada, g_pre_mix, g_post_mix, g_pre_mlp, g_post_mlp, w_in, conv_w,
           w_proj_conv, w_proj_attn, w_out, w_mlp_in, w_mlp_out):
    depth = w_ada.shape[0]
    mod = _ada(c, w_ada, b_ada)
    w_mlp_out = w_mlp_out.astype(_bf16)
    for l in range(depth):
        q, k, v, gc, sgb = _mixer_in(l, x, mod[l], g_pre_mix[l], w_in, conv_w, w_proj_conv)
        o = _attention(q, k, v)
        gains = jnp.stack([g_post_mix[l], g_pre_mlp[l], g_post_mlp[l]])
        x = _mixer_out(l, x, o, gc, sgb, mod[l], gains, w_proj_attn, w_out, w_mlp_in, w_mlp_out)
    return x
```

```python
import functools
import math

import jax
import jax.numpy as jnp
from jax import lax
from jax.experimental import pallas as pl
from jax.experimental.pallas import tpu as pltpu

CONV_WIDTH = 512
CONV_TAPS = 3
ATTN_WIDTH = 512
HEAD_DIM = 64
N_MOD = 6
EPS = 1e-6

LANES = 128
SUBLANES = 8
HEADS_PER_TILE = LANES // HEAD_DIM
ATTN_LANE_TILES = 4
SEQ_TILE = 512
Q_TILE = 256
K_TILE = 256
ADA_COLS = 2048
FF_CHUNK = 1024
ROW_SUBTILES = 2
VMEM_LIMIT = 56 * 1024 * 1024
LOG2E = math.log2(math.e)
UNDERFLOW_LOG2 = 152.0
FUSED_BLOCKS = 3
MASKED_LOGIT = -1e30
Z_CLAMP = 120.0

_f32 = jnp.float32
_bf16 = jnp.bfloat16


def _dot(a, b):
    return jnp.dot(a, b, preferred_element_type=_f32)


def _rms(x, gain):
    return x * lax.rsqrt(jnp.mean(x * x, axis=-1, keepdims=True) + EPS) * gain


def _layer_spec(stacked, layer):
    rest = stacked.shape[1:]
    return pl.BlockSpec((None,) + rest, lambda *_: (layer,) + (0,) * len(rest),
                        pipeline_mode=pl.Buffered(1))


def _const_spec(shape):
    return pl.BlockSpec(shape, lambda *_: (0,) * len(shape), pipeline_mode=pl.Buffered(1))


def _ada_kernel(ct_ref, w_ref, b_ref, o_ref):
    w = w_ref[0]
    rows = [jnp.sum(w * ct_ref[:, i:i + 1], axis=0, keepdims=True) for i in range(ct_ref.shape[1])]
    o_ref[0] = jnp.concatenate(rows, axis=0) + b_ref[0, 0]


def _ada(c, w_ada, b_ada):
    depth, d, width = w_ada.shape
    b = c.shape[0]
    cols = ADA_COLS
    return pl.pallas_call(
        _ada_kernel,
        grid=(depth, width // cols),
        in_specs=[
            pl.BlockSpec((d, b), lambda l, j: (0, 0)),
            pl.BlockSpec((1, d, cols), lambda l, j: (l, 0, j)),
            pl.BlockSpec((1, 1, 1, cols), lambda l, j: (l, j, 0, 0)),
        ],
        out_specs=pl.BlockSpec((1, b, cols), lambda l, j: (l, 0, j)),
        out_shape=jax.ShapeDtypeStruct((depth, b, width), _f32),
        compiler_params=pltpu.CompilerParams(
            dimension_semantics=("arbitrary", "arbitrary"), vmem_limit_bytes=VMEM_LIMIT),
        name="ada",
    )(c.T, w_ada, b_ada.reshape(depth, width // cols, 1, cols))


def _mixer_in_kernel(x_ref, mod_ref, g_ref, w_in_ref, cw_ref, wpc_ref,
                     q_ref, k_ref, v_ref, gc_ref, sgb_ref, carry_ref):
    d = x_ref.shape[-1]
    tm = x_ref.shape[1]
    cw = CONV_WIDTH

    @pl.when(pl.program_id(1) == 0)
    def _():
        carry_ref[...] = jnp.zeros_like(carry_ref)

    shift = mod_ref[0, :, 0:d]
    scale = mod_ref[0, :, d:2 * d]
    h = _rms(x_ref[0], g_ref[...]) * (1.0 + scale) + shift

    bg = _dot(h, w_in_ref[:, 0:cw])
    cg = _dot(h, w_in_ref[:, cw:2 * cw])
    u = _dot(h, w_in_ref[:, 2 * cw:3 * cw])

    o = 3 * cw
    aw = ATTN_WIDTH
    q_ref[0] = (_dot(h, w_in_ref[:, o:o + aw]) * (LOG2E / math.sqrt(HEAD_DIM))).astype(_bf16)
    k_ref[0] = _dot(h, w_in_ref[:, o + aw:o + 2 * aw]).astype(_bf16)
    v_ref[0] = _dot(h, w_in_ref[:, o + 2 * aw:o + 3 * aw]).astype(_bf16)
    o += 3 * aw
    gb = _dot(h, w_in_ref[:, o + d:o + 2 * d])
    sgb_ref[0] = jax.nn.sigmoid(gb)
    ga = _dot(h, w_in_ref[:, o:o + d])

    vv = cg * u
    row = lax.broadcasted_iota(jnp.int32, vv.shape, 0)
    prev1 = carry_ref[1:2, :]
    prev2 = carry_ref[0:1, :]
    v1 = jnp.where(row == 0, prev1, pltpu.roll(vv, 1, axis=0))
    v2 = jnp.where(row == 0, prev2, jnp.where(row == 1, prev1, pltpu.roll(vv, 2, axis=0)))
    carry_ref[...] = vv[tm - 2:tm, :]
    conv = cw_ref[0:1, :] * v2 + cw_ref[1:2, :] * v1 + cw_ref[2:3, :] * vv
    y_conv = _dot(bg * conv, wpc_ref[...])
    gc_ref[0] = jax.nn.sigmoid(ga) * y_conv


def _mixer_in(layer, x, mod, g_pre, w_in, conv_w, w_proj_conv):
    b, s, d = x.shape
    tm = SEQ_TILE
    tile = lambda width: pl.BlockSpec((1, tm, width), lambda i, j: (i, j, 0))
    return pl.pallas_call(
        _mixer_in_kernel,
        grid=(b, s // tm),
        in_specs=[
            tile(d),
            pl.BlockSpec((1, 1, N_MOD * d), lambda i, j: (i, 0, 0)),
            _const_spec((1, d)),
            _layer_spec(w_in, layer),
            _layer_spec(conv_w, layer),
            _layer_spec(w_proj_conv, layer),
        ],
        out_specs=[tile(ATTN_WIDTH), tile(ATTN_WIDTH), tile(ATTN_WIDTH), tile(d), tile(d)],
        out_shape=[
            jax.ShapeDtypeStruct((b, s, ATTN_WIDTH), _bf16),
            jax.ShapeDtypeStruct((b, s, ATTN_WIDTH), _bf16),
            jax.ShapeDtypeStruct((b, s, ATTN_WIDTH), _bf16),
            jax.ShapeDtypeStruct((b, s, d), _f32),
            jax.ShapeDtypeStruct((b, s, d), _f32),
        ],
        scratch_shapes=[pltpu.VMEM((CONV_TAPS - 1, CONV_WIDTH), _f32)],
        compiler_params=pltpu.CompilerParams(
            dimension_semantics=("arbitrary", "arbitrary"), vmem_limit_bytes=VMEM_LIMIT),
        name="mixer_in",
    )(x, mod.reshape(b, 1, N_MOD * d), g_pre.reshape(1, d), w_in, conv_w, w_proj_conv)


def _attn_kernel(q_ref, k_ref, v_ref, o_ref, qh_ref, acc_ref, run_ref):
    tq = q_ref.shape[1]
    qi = pl.program_id(2)
    n_heads = qh_ref.shape[0]
    lanes_of = lambda h: slice((h // HEADS_PER_TILE) * LANES, (h // HEADS_PER_TILE + 1) * LANES)

    lane = lax.broadcasted_iota(jnp.int32, (tq, LANES), 1)
    own = [lane < HEAD_DIM, lane >= HEAD_DIM]
    for h in range(n_heads):
        q = q_ref[0, :, lanes_of(h)]
        qh_ref[h] = jnp.where(own[h % HEADS_PER_TILE], q, jnp.zeros_like(q))

    tri2 = (lax.broadcasted_iota(jnp.int32, (K_TILE, K_TILE), 0)
            >= lax.broadcasted_iota(jnp.int32, (K_TILE, K_TILE), 1)).astype(_bf16)

    heads = range(n_heads)

    def rows_of(kb):
        return pl.ds(pl.multiple_of(kb * K_TILE, K_TILE), K_TILE)

    def logits(kb, h):
        return lax.dot_general(qh_ref[h], k_ref[0, rows_of(kb), lanes_of(h)],
                               (((1,), (1,)), ((), ())), preferred_element_type=_f32)

    def suffix_sums(z, diagonal):
        sp = jnp.maximum(jnp.log2(1.0 + jnp.exp2(jnp.minimum(z, Z_CLAMP))), z)
        if diagonal:
            keep = (lax.broadcasted_iota(jnp.int32, z.shape, 1)
                    < lax.broadcasted_iota(jnp.int32, z.shape, 0))
            sp = jnp.where(keep, sp, 0.0)
            z = jnp.where(keep, z, MASKED_LOGIT)
        return z, _dot(sp.astype(_bf16), tri2)

    def accumulate(kb, h, z, suffix, first):
        block_sum = jnp.broadcast_to(suffix[:, 0:1], (tq, LANES))
        values = v_ref[0, rows_of(kb), lanes_of(h)]
        if first:
            acc_ref[h] = _dot(jnp.exp2(z - suffix).astype(_bf16), values)
            run_ref[h] = block_sum
        else:
            run = run_ref[h]
            a = jnp.exp2(z - suffix - jnp.concatenate([run] * (K_TILE // LANES), axis=1))
            acc_ref[h] += _dot(a.astype(_bf16), values)
            run_ref[h] = run + block_sum

    def run_min():
        parts = [run_ref[h] for h in heads]
        while len(parts) > 1:
            parts = [jnp.minimum(a, b) for a, b in zip(parts[0::2], parts[1::2])]
        m = parts[0]
        while m.shape[0] > SUBLANES:
            half = m.shape[0] // 2
            m = jnp.minimum(m[:half], m[half:])
        return jnp.min(m)

    def sweep(blocks):
        n = len(blocks)
        z = [None] * n
        suffix = [None] * n
        for phase in range(n + 2):
            for h in heads:
                if 0 <= phase - 2 < n:
                    accumulate(blocks[phase - 2], h, z[phase - 2][h], suffix[phase - 2][h],
                               first=(phase == 2))
                if 0 <= phase - 1 < n:
                    z[phase - 1][h], s = suffix_sums(z[phase - 1][h], diagonal=(phase == 1))
                    suffix[phase - 1] = (suffix[phase - 1] or []) + [s]
                if phase < n:
                    z[phase] = (z[phase] or []) + [logits(blocks[phase], h)]

    @pl.when(qi >= FUSED_BLOCKS - 1)
    def _():
        sweep([qi - i for i in range(FUSED_BLOCKS)])

    @pl.when(qi < FUSED_BLOCKS - 1)
    def _():
        sweep([qi])

    def more(carry):
        kb, run_min = carry
        return jnp.logical_and(kb >= 0, run_min < UNDERFLOW_LOG2)

    def body(carry):
        kb, _ = carry
        z = [logits(kb, h) for h in heads]
        suffix = [suffix_sums(z[h], diagonal=False)[1] for h in heads]
        for h in heads:
            accumulate(kb, h, z[h], suffix[h], first=False)
        return kb - 1, run_min()

    first_left = jnp.where(qi >= FUSED_BLOCKS - 1, qi - FUSED_BLOCKS, qi - 1)
    lax.while_loop(more, body, (first_left, run_min()))

    for h in range(0, n_heads, HEADS_PER_TILE):
        o_ref[0, :, lanes_of(h)] = jnp.where(own[0], acc_ref[h], acc_ref[h + 1]).astype(o_ref.dtype)


def _attention(q, k, v):
    b, s, w = q.shape
    width = ATTN_LANE_TILES * LANES
    n_heads = ATTN_LANE_TILES * HEADS_PER_TILE
    assert Q_TILE == K_TILE and s % Q_TILE == 0 and w % width == 0
    seq = pl.BlockSpec((1, s, width), lambda i, h, j: (i, 0, h))
    tile = pl.BlockSpec((1, Q_TILE, width), lambda i, h, j: (i, j, h))
    return pl.pallas_call(
        _attn_kernel,
        grid=(b, w // width, s // Q_TILE),
        in_specs=[tile, seq, seq],
        out_specs=tile,
        out_shape=jax.ShapeDtypeStruct((b, s, w), _bf16),
        scratch_shapes=[
            pltpu.VMEM((n_heads, Q_TILE, LANES), _bf16),
            pltpu.VMEM((n_heads, Q_TILE, LANES), _f32),
            pltpu.VMEM((n_heads, Q_TILE, LANES), _f32),
        ],
        compiler_params=pltpu.CompilerParams(
            dimension_semantics=("arbitrary", "arbitrary", "arbitrary"),
            vmem_limit_bytes=VMEM_LIMIT),
        name="attention",
    )(q, k, v)


def _mixer_out_kernel(x_ref, o_ref, gc_ref, sgb_ref, mod_ref, gains_ref,
                      wpa_ref, wout_ref, w1_ref, w2_ref, out_ref):
    d = x_ref.shape[-1]
    mod = lambda i: mod_ref[0, :, i * d:(i + 1) * d]
    gate1, shift2, scale2, gate2 = mod(2), mod(3), mod(4), mod(5)
    g_post_mix = gains_ref[0:1, :]
    g_pre_mlp = gains_ref[1:2, :]
    g_post_mlp = gains_ref[2:3, :]

    tm = x_ref.shape[1]
    subs = [pl.ds(r, tm // ROW_SUBTILES) for r in range(0, tm, tm // ROW_SUBTILES)]

    mix = []
    for rows in subs:
        y_attn = _dot(o_ref[0, rows, :], wpa_ref[...])
        merged = gc_ref[0, rows, :] + sgb_ref[0, rows, :] * y_attn
        mix.append(_dot(merged, wout_ref[...]))
    x1, h2 = [], []
    for i, rows in enumerate(subs):
        x1.append(x_ref[0, rows, :] + gate1 * _rms(mix[i], g_post_mix))
        h2.append(_rms(x1[i], g_pre_mlp) * (1.0 + scale2) + shift2)
    for i, rows in enumerate(subs):
        ff = jnp.zeros_like(x1[i])
        for j in range(0, w1_ref.shape[1], FF_CHUNK):
            hid = jnp.maximum(_dot(h2[i], w1_ref[:, j:j + FF_CHUNK]), 0.0)
            ff = ff + _dot((hid * hid).astype(_bf16), w2_ref[j:j + FF_CHUNK, :])
        out_ref[0, rows, :] = x1[i] + gate2 * _rms(ff, g_post_mlp)


def _mixer_out(layer, x, o, gc, sgb, mod, gains, w_proj_attn, w_out, w1, w2):
    b, s, d = x.shape
    tm = SEQ_TILE
    tile = lambda width: pl.BlockSpec((1, tm, width), lambda i, j: (i, j, 0))
    return pl.pallas_call(
        _mixer_out_kernel,
        grid=(b, s // tm),
        in_specs=[
            tile(d), tile(ATTN_WIDTH), tile(d), tile(d),
            pl.BlockSpec((1, 1, N_MOD * d), lambda i, j: (i, 0, 0)),
            _const_spec(gains.shape),
            _layer_spec(w_proj_attn, layer),
            _layer_spec(w_out, layer),
            _layer_spec(w1, layer),
            _layer_spec(w2, layer),
        ],
        out_specs=tile(d),
        out_shape=jax.ShapeDtypeStruct((b, s, d), _f32),
        compiler_params=pltpu.CompilerParams(
            dimension_semantics=("arbitrary", "arbitrary"), vmem_limit_bytes=VMEM_LIMIT),
        name="mixer_out",
    )(x, o, gc, sgb, mod.reshape(b, 1, N_MOD * d), gains, w_proj_attn, w_out, w1, w2)


def kernel(x, c, w_ada, b_ada, g_pre_mix, g_post_mix, g_pre_mlp, g_post_mlp, w_in, conv_w,
           w_proj_conv, w_proj_attn, w_out, w_mlp_in, w_mlp_out):
    depth = w_ada.shape[0]
    mod = _ada(c, w_ada, b_ada)
    w_mlp_out = w_mlp_out.astype(_bf16)
    for l in range(depth):
        q, k, v, gc, sgb = _mixer_in(l, x, mod[l], g_pre_mix[l], w_in, conv_w, w_proj_conv)
        o = _attention(q, k, v)
        gains = jnp.stack([g_post_mix[l], g_pre_mlp[l], g_post_mlp[l]])
        x = _mixer_out(l, x, o, gc, sgb, mod[l], gains, w_proj_attn, w_out, w_mlp_in, w_mlp_out)
    return x
```

```python
import functools
import math

import jax
import jax.numpy as jnp
from jax import lax
from jax.experimental import pallas as pl
from jax.experimental.pallas import tpu as pltpu

CONV_WIDTH = 512
CONV_TAPS = 3
ATTN_WIDTH = 512
HEAD_DIM = 64
N_MOD = 6
EPS = 1e-6

LANES = 128
SUBLANES = 8
HEADS_PER_TILE = LANES // HEAD_DIM
ATTN_LANE_TILES = 4
SEQ_TILE = 512
Q_TILE = 256
K_TILE = 256
ADA_COLS = 2048
FF_CHUNK = 1024
ROW_SUBTILES = 2
VMEM_LIMIT = 56 * 1024 * 1024
LOG2E = math.log2(math.e)
UNDERFLOW_LOG2 = 152.0
FUSED_BLOCKS = 3
TAIL_ROW_GROUPS = 2
MASKED_LOGIT = -1e30
Z_CLAMP = 120.0

_f32 = jnp.float32
_bf16 = jnp.bfloat16


def _dot(a, b):
    return jnp.dot(a, b, preferred_element_type=_f32)


def _rms(x, gain):
    return x * lax.rsqrt(jnp.mean(x * x, axis=-1, keepdims=True) + EPS) * gain


def _layer_spec(stacked, layer):
    rest = stacked.shape[1:]
    return pl.BlockSpec((None,) + rest, lambda *_: (layer,) + (0,) * len(rest),
                        pipeline_mode=pl.Buffered(1))


def _const_spec(shape):
    return pl.BlockSpec(shape, lambda *_: (0,) * len(shape), pipeline_mode=pl.Buffered(1))


def _ada_kernel(ct_ref, w_ref, b_ref, o_ref):
    w = w_ref[0]
    rows = [jnp.sum(w * ct_ref[:, i:i + 1], axis=0, keepdims=True) for i in range(ct_ref.shape[1])]
    o_ref[0] = jnp.concatenate(rows, axis=0) + b_ref[0, 0]


def _ada(c, w_ada, b_ada):
    depth, d, width = w_ada.shape
    b = c.shape[0]
    cols = ADA_COLS
    return pl.pallas_call(
        _ada_kernel,
        grid=(depth, width // cols),
        in_specs=[
            pl.BlockSpec((d, b), lambda l, j: (0, 0)),
            pl.BlockSpec((1, d, cols), lambda l, j: (l, 0, j)),
            pl.BlockSpec((1, 1, 1, cols), lambda l, j: (l, j, 0, 0)),
        ],
        out_specs=pl.BlockSpec((1, b, cols), lambda l, j: (l, 0, j)),
        out_shape=jax.ShapeDtypeStruct((depth, b, width), _f32),
        compiler_params=pltpu.CompilerParams(
            dimension_semantics=("arbitrary", "arbitrary"), vmem_limit_bytes=VMEM_LIMIT),
        name="ada",
    )(c.T, w_ada, b_ada.reshape(depth, width // cols, 1, cols))


def _mixer_in_kernel(x_ref, mod_ref, g_ref, w_in_ref, cw_ref, wpc_ref,
                     q_ref, k_ref, v_ref, gc_ref, sgb_ref, carry_ref):
    d = x_ref.shape[-1]
    tm = x_ref.shape[1]
    cw = CONV_WIDTH

    @pl.when(pl.program_id(1) == 0)
    def _():
        carry_ref[...] = jnp.zeros_like(carry_ref)

    shift = mod_ref[0, :, 0:d]
    scale = mod_ref[0, :, d:2 * d]
    h = _rms(x_ref[0], g_ref[...]) * (1.0 + scale) + shift

    bg = _dot(h, w_in_ref[:, 0:cw])
    cg = _dot(h, w_in_ref[:, cw:2 * cw])
    u = _dot(h, w_in_ref[:, 2 * cw:3 * cw])

    o = 3 * cw
    aw = ATTN_WIDTH
    q_ref[0] = (_dot(h, w_in_ref[:, o:o + aw]) * (LOG2E / math.sqrt(HEAD_DIM))).astype(_bf16)
    k_ref[0] = _dot(h, w_in_ref[:, o + aw:o + 2 * aw]).astype(_bf16)
    v_ref[0] = _dot(h, w_in_ref[:, o + 2 * aw:o + 3 * aw]).astype(_bf16)
    o += 3 * aw
    gb = _dot(h, w_in_ref[:, o + d:o + 2 * d])
    sgb_ref[0] = jax.nn.sigmoid(gb)
    ga = _dot(h, w_in_ref[:, o:o + d])

    vv = cg * u
    row = lax.broadcasted_iota(jnp.int32, vv.shape, 0)
    prev1 = carry_ref[1:2, :]
    prev2 = carry_ref[0:1, :]
    v1 = jnp.where(row == 0, prev1, pltpu.roll(vv, 1, axis=0))
    v2 = jnp.where(row == 0, prev2, jnp.where(row == 1, prev1, pltpu.roll(vv, 2, axis=0)))
    carry_ref[...] = vv[tm - 2:tm, :]
    conv = cw_ref[0:1, :] * v2 + cw_ref[1:2, :] * v1 + cw_ref[2:3, :] * vv
    y_conv = _dot(bg * conv, wpc_ref[...])
    gc_ref[0] = jax.nn.sigmoid(ga) * y_conv


def _mixer_in(layer, x, mod, g_pre, w_in, conv_w, w_proj_conv):
    b, s, d = x.shape
    tm = SEQ_TILE
    tile = lambda width: pl.BlockSpec((1, tm, width), lambda i, j: (i, j, 0))
    return pl.pallas_call(
        _mixer_in_kernel,
        grid=(b, s // tm),
        in_specs=[
            tile(d),
            pl.BlockSpec((1, 1, N_MOD * d), lambda i, j: (i, 0, 0)),
            _const_spec((1, d)),
            _layer_spec(w_in, layer),
            _layer_spec(conv_w, layer),
            _layer_spec(w_proj_conv, layer),
        ],
        out_specs=[tile(ATTN_WIDTH), tile(ATTN_WIDTH), tile(ATTN_WIDTH), tile(d), tile(d)],
        out_shape=[
            jax.ShapeDtypeStruct((b, s, ATTN_WIDTH), _bf16),
            jax.ShapeDtypeStruct((b, s, ATTN_WIDTH), _bf16),
            jax.ShapeDtypeStruct((b, s, ATTN_WIDTH), _bf16),
            jax.ShapeDtypeStruct((b, s, d), _f32),
            jax.ShapeDtypeStruct((b, s, d), _f32),
        ],
        scratch_shapes=[pltpu.VMEM((CONV_TAPS - 1, CONV_WIDTH), _f32)],
        compiler_params=pltpu.CompilerParams(
            dimension_semantics=("arbitrary", "arbitrary"), vmem_limit_bytes=VMEM_LIMIT),
        name="mixer_in",
    )(x, mod.reshape(b, 1, N_MOD * d), g_pre.reshape(1, d), w_in, conv_w, w_proj_conv)


def _attn_kernel(q_ref, k_ref, v_ref, o_ref, qh_ref, acc_ref, run_ref):
    tq = q_ref.shape[1]
    qi = pl.program_id(2)
    n_heads = qh_ref.shape[0]
    lanes_of = lambda h: slice((h // HEADS_PER_TILE) * LANES, (h // HEADS_PER_TILE + 1) * LANES)

    lane = lax.broadcasted_iota(jnp.int32, (tq, LANES), 1)
    own = [lane < HEAD_DIM, lane >= HEAD_DIM]
    for h in range(n_heads):
        q = q_ref[0, :, lanes_of(h)]
        qh_ref[h] = jnp.where(own[h % HEADS_PER_TILE], q, jnp.zeros_like(q))

    tri2 = (lax.broadcasted_iota(jnp.int32, (K_TILE, K_TILE), 0)
            >= lax.broadcasted_iota(jnp.int32, (K_TILE, K_TILE), 1)).astype(_bf16)

    heads = range(n_heads)

    def rows_of(kb):
        return pl.ds(pl.multiple_of(kb * K_TILE, K_TILE), K_TILE)

    every_row = slice(0, tq)

    def logits(kb, h, rows=every_row):
        return lax.dot_general(qh_ref[h, rows, :], k_ref[0, rows_of(kb), lanes_of(h)],
                               (((1,), (1,)), ((), ())), preferred_element_type=_f32)

    def suffix_sums(z, diagonal):
        sp = jnp.maximum(jnp.log2(1.0 + jnp.exp2(jnp.minimum(z, Z_CLAMP))), z)
        if diagonal:
            keep = (lax.broadcasted_iota(jnp.int32, z.shape, 1)
                    < lax.broadcasted_iota(jnp.int32, z.shape, 0))
            sp = jnp.where(keep, sp, 0.0)
            z = jnp.where(keep, z, MASKED_LOGIT)
        return z, _dot(sp.astype(_bf16), tri2)

    def accumulate(kb, h, z, suffix, first, rows=every_row):
        block_sum = jnp.broadcast_to(suffix[:, 0:1], (z.shape[0], LANES))
        values = v_ref[0, rows_of(kb), lanes_of(h)]
        if first:
            acc_ref[h, rows, :] = _dot(jnp.exp2(z - suffix).astype(_bf16), values)
            run_ref[h, rows, :] = block_sum
        else:
            run = run_ref[h, rows, :]
            a = jnp.exp2(z - suffix - jnp.concatenate([run] * (K_TILE // LANES), axis=1))
            acc_ref[h, rows, :] += _dot(a.astype(_bf16), values)
            run_ref[h, rows, :] = run + block_sum

    def run_min(rows):
        parts = [run_ref[h, rows, :] for h in heads]
        while len(parts) > 1:
            parts = [jnp.minimum(a, b) for a, b in zip(parts[0::2], parts[1::2])]
        m = parts[0]
        while m.shape[0] > SUBLANES:
            half = m.shape[0] // 2
            m = jnp.minimum(m[:half], m[half:])
        return jnp.min(m)

    def sweep(blocks):
        n = len(blocks)
        z = [None] * n
        suffix = [None] * n
        for phase in range(n + 2):
            for h in heads:
                if 0 <= phase - 2 < n:
                    accumulate(blocks[phase - 2], h, z[phase - 2][h], suffix[phase - 2][h],
                               first=(phase == 2))
                if 0 <= phase - 1 < n:
                    z[phase - 1][h], s = suffix_sums(z[phase - 1][h], diagonal=(phase == 1))
                    suffix[phase - 1] = (suffix[phase - 1] or []) + [s]
                if phase < n:
                    z[phase] = (z[phase] or []) + [logits(blocks[phase], h)]

    @pl.when(qi >= FUSED_BLOCKS - 1)
    def _():
        sweep([qi - i for i in range(FUSED_BLOCKS)])

    @pl.when(qi < FUSED_BLOCKS - 1)
    def _():
        sweep([qi])

    row_groups = [slice(r, r + tq // TAIL_ROW_GROUPS) for r in range(0, tq, tq // TAIL_ROW_GROUPS)]

    def run_mins():
        return tuple(run_min(rows) for rows in row_groups)

    def more(carry):
        kb, mins = carry
        return jnp.logical_and(kb >= 0, functools.reduce(jnp.minimum, mins) < UNDERFLOW_LOG2)

    def body(carry):
        kb, mins = carry
        for rows, group_min in zip(row_groups, mins):
            @pl.when(group_min < UNDERFLOW_LOG2)
            def _():
                z = [logits(kb, h, rows) for h in heads]
                suffix = [suffix_sums(z[h], diagonal=False)[1] for h in heads]
                for h in heads:
                    accumulate(kb, h, z[h], suffix[h], first=False, rows=rows)
        return kb - 1, run_mins()

    first_left = jnp.where(qi >= FUSED_BLOCKS - 1, qi - FUSED_BLOCKS, qi - 1)
    lax.while_loop(more, body, (first_left, run_mins()))

    for h in range(0, n_heads, HEADS_PER_TILE):
        o_ref[0, :, lanes_of(h)] = jnp.where(own[0], acc_ref[h], acc_ref[h + 1]).astype(o_ref.dtype)


def _attention(q, k, v):
    b, s, w = q.shape
    width = ATTN_LANE_TILES * LANES
    n_heads = ATTN_LANE_TILES * HEADS_PER_TILE
    assert Q_TILE == K_TILE and s % Q_TILE == 0 and w % width == 0
    seq = pl.BlockSpec((1, s, width), lambda i, h, j: (i, 0, h))
    tile = pl.BlockSpec((1, Q_TILE, width), lambda i, h, j: (i, j, h))
    return pl.pallas_call(
        _attn_kernel,
        grid=(b, w // width, s // Q_TILE),
        in_specs=[tile, seq, seq],
        out_specs=tile,
        out_shape=jax.ShapeDtypeStruct((b, s, w), _bf16),
        scratch_shapes=[
            pltpu.VMEM((n_heads, Q_TILE, LANES), _bf16),
            pltpu.VMEM((n_heads, Q_TILE, LANES), _f32),
            pltpu.VMEM((n_heads, Q_TILE, LANES), _f32),
        ],
        compiler_params=pltpu.CompilerParams(
            dimension_semantics=("arbitrary", "arbitrary", "arbitrary"),
            vmem_limit_bytes=VMEM_LIMIT),
        name="attention",
    )(q, k, v)


def _mixer_out_kernel(x_ref, o_ref, gc_ref, sgb_ref, mod_ref, gains_ref,
                      wpa_ref, wout_ref, w1_ref, w2_ref, out_ref):
    d = x_ref.shape[-1]
    mod = lambda i: mod_ref[0, :, i * d:(i + 1) * d]
    gate1, shift2, scale2, gate2 = mod(2), mod(3), mod(4), mod(5)
    g_post_mix = gains_ref[0:1, :]
    g_pre_mlp = gains_ref[1:2, :]
    g_post_mlp = gains_ref[2:3, :]

    tm = x_ref.shape[1]
    subs = [pl.ds(r, tm // ROW_SUBTILES) for r in range(0, tm, tm // ROW_SUBTILES)]

    mix = []
    for rows in subs:
        y_attn = _dot(o_ref[0, rows, :], wpa_ref[...])
        merged = gc_ref[0, rows, :] + sgb_ref[0, rows, :] * y_attn
        mix.append(_dot(merged, wout_ref[...]))
    x1, h2 = [], []
    for i, rows in enumerate(subs):
        x1.append(x_ref[0, rows, :] + gate1 * _rms(mix[i], g_post_mix))
        h2.append(_rms(x1[i], g_pre_mlp) * (1.0 + scale2) + shift2)
    for i, rows in enumerate(subs):
        ff = jnp.zeros_like(x1[i])
        for j in range(0, w1_ref.shape[1], FF_CHUNK):
            hid = jnp.maximum(_dot(h2[i], w1_ref[:, j:j + FF_CHUNK]), 0.0)
            ff = ff + _dot((hid * hid).astype(_bf16), w2_ref[j:j + FF_CHUNK, :])
        out_ref[0, rows, :] = x1[i] + gate2 * _rms(ff, g_post_mlp)


def _mixer_out(layer, x, o, gc, sgb, mod, gains, w_proj_attn, w_out, w1, w2):
    b, s, d = x.shape
    tm = SEQ_TILE
    tile = lambda width: pl.BlockSpec((1, tm, width), lambda i, j: (i, j, 0))
    return pl.pallas_call(
        _mixer_out_kernel,
        grid=(b, s // tm),
        in_specs=[
            tile(d), tile(ATTN_WIDTH), tile(d), tile(d),
            pl.BlockSpec((1, 1, N_MOD * d), lambda i, j: (i, 0, 0)),
            _const_spec(gains.shape),
            _layer_spec(w_proj_attn, layer),
            _layer_spec(w_out, layer),
            _layer_spec(w1, layer),
            _layer_spec(w2, layer),
        ],
        out_specs=tile(d),
        out_shape=jax.ShapeDtypeStruct((b, s, d), _f32),
        compiler_params=pltpu.CompilerParams(
            dimension_semantics=("arbitrary", "arbitrary"), vmem_limit_bytes=VMEM_LIMIT),
        name="mixer_out",
    )(x, o, gc, sgb, mod.reshape(b, 1, N_MOD * d), gains, w_proj_attn, w_out, w1, w2)


def kernel(x, c, w_ada, b_ada, g_pre_mix, g_post_mix, g_pre_mlp, g_post_mlp, w_in, conv_w,
           w_proj_conv, w_proj_attn, w_out, w_mlp_in, w_mlp_out):
    depth = w_ada.shape[0]
    mod = _ada(c, w_ada, b_ada)
    w_mlp_out = w_mlp_out.astype(_bf16)
    for l in range(depth):
        q, k, v, gc, sgb = _mixer_in(l, x, mod[l], g_pre_mix[l], w_in, conv_w, w_proj_conv)
        o = _attention(q, k, v)
        gains = jnp.stack([g_post_mix[l], g_pre_mlp[l], g_post_mlp[l]])
        x = _mixer_out(l, x, o, gc, sgb, mod[l], gains, w_proj_attn, w_out, w_mlp_in, w_mlp_out)
    return x
```

```python
import functools
import math

import jax
import jax.numpy as jnp
from jax import lax
from jax.experimental import pallas as pl
from jax.experimental.pallas import tpu as pltpu

CONV_WIDTH = 512
CONV_TAPS = 3
ATTN_WIDTH = 512
HEAD_DIM = 64
N_MOD = 6
EPS = 1e-6

LANES = 128
SUBLANES = 8
HEADS_PER_TILE = LANES // HEAD_DIM
ATTN_LANE_TILES = 4
SEQ_TILE = 512
Q_TILE = 256
K_TILE = 256
ADA_COLS = 2048
FF_CHUNK = 1024
ROW_SUBTILES = 2
VMEM_LIMIT = 56 * 1024 * 1024
LOG2E = math.log2(math.e)
UNDERFLOW_LOG2 = 152.0
FUSED_BLOCKS = 3
TAIL_ROW_GROUPS = 2
MASKED_LOGIT = -1e30
Z_CLAMP = 120.0

_f32 = jnp.float32
_bf16 = jnp.bfloat16


def _dot(a, b):
    return jnp.dot(a, b, preferred_element_type=_f32)


def _rms(x, gain):
    return x * lax.rsqrt(jnp.mean(x * x, axis=-1, keepdims=True) + EPS) * gain


def _layer_spec(stacked, layer):
    rest = stacked.shape[1:]
    return pl.BlockSpec((None,) + rest, lambda *_: (layer,) + (0,) * len(rest),
                        pipeline_mode=pl.Buffered(1))


def _const_spec(shape):
    return pl.BlockSpec(shape, lambda *_: (0,) * len(shape), pipeline_mode=pl.Buffered(1))


def _ada_kernel(ct_ref, w_ref, b_ref, o_ref):
    w = w_ref[0]
    rows = [jnp.sum(w * ct_ref[:, i:i + 1], axis=0, keepdims=True) for i in range(ct_ref.shape[1])]
    o_ref[0] = jnp.concatenate(rows, axis=0) + b_ref[0, 0]


def _ada(c, w_ada, b_ada):
    depth, d, width = w_ada.shape
    b = c.shape[0]
    cols = ADA_COLS
    return pl.pallas_call(
        _ada_kernel,
        grid=(depth, width // cols),
        in_specs=[
            pl.BlockSpec((d, b), lambda l, j: (0, 0)),
            pl.BlockSpec((1, d, cols), lambda l, j: (l, 0, j)),
            pl.BlockSpec((1, 1, 1, cols), lambda l, j: (l, j, 0, 0)),
        ],
        out_specs=pl.BlockSpec((1, b, cols), lambda l, j: (l, 0, j)),
        out_shape=jax.ShapeDtypeStruct((depth, b, width), _f32),
        compiler_params=pltpu.CompilerParams(
            dimension_semantics=("arbitrary", "arbitrary"), vmem_limit_bytes=VMEM_LIMIT),
        name="ada",
    )(c.T, w_ada, b_ada.reshape(depth, width // cols, 1, cols))


def _mixer_in_kernel(x_ref, mod_ref, g_ref, w_in_ref, cw_ref, wpc_ref,
                     q_ref, k_ref, v_ref, gc_ref, sgb_ref, carry_ref):
    d = x_ref.shape[-1]
    tm = x_ref.shape[1]
    cw = CONV_WIDTH

    @pl.when(pl.program_id(1) == 0)
    def _():
        carry_ref[...] = jnp.zeros_like(carry_ref)

    shift = mod_ref[0, :, 0:d]
    scale = mod_ref[0, :, d:2 * d]
    h = _rms(x_ref[0], g_ref[...]) * (1.0 + scale) + shift

    bg = _dot(h, w_in_ref[:, 0:cw])
    cg = _dot(h, w_in_ref[:, cw:2 * cw])
    u = _dot(h, w_in_ref[:, 2 * cw:3 * cw])

    o = 3 * cw
    aw = ATTN_WIDTH
    q_ref[0] = (_dot(h, w_in_ref[:, o:o + aw]) * (LOG2E / math.sqrt(HEAD_DIM))).astype(_bf16)
    k_ref[0] = _dot(h, w_in_ref[:, o + aw:o + 2 * aw]).astype(_bf16)
    v_ref[0] = _dot(h, w_in_ref[:, o + 2 * aw:o + 3 * aw]).astype(_bf16)
    o += 3 * aw
    gb = _dot(h, w_in_ref[:, o + d:o + 2 * d])
    sgb_ref[0] = jax.nn.sigmoid(gb)
    ga = _dot(h, w_in_ref[:, o:o + d])

    vv = cg * u
    row = lax.broadcasted_iota(jnp.int32, vv.shape, 0)
    prev1 = carry_ref[1:2, :]
    prev2 = carry_ref[0:1, :]
    v1 = jnp.where(row == 0, prev1, pltpu.roll(vv, 1, axis=0))
    v2 = jnp.where(row == 0, prev2, jnp.where(row == 1, prev1, pltpu.roll(vv, 2, axis=0)))
    carry_ref[...] = vv[tm - 2:tm, :]
    conv = cw_ref[0:1, :] * v2 + cw_ref[1:2, :] * v1 + cw_ref[2:3, :] * vv
    y_conv = _dot(bg * conv, wpc_ref[...])
    gc_ref[0] = jax.nn.sigmoid(ga) * y_conv


def _mixer_in(layer, x, mod, g_pre, w_in, conv_w, w_proj_conv):
    b, s, d = x.shape
    tm = SEQ_TILE
    tile = lambda width: pl.BlockSpec((1, tm, width), lambda i, j: (i, j, 0))
    return pl.pallas_call(
        _mixer_in_kernel,
        grid=(b, s // tm),
        in_specs=[
            tile(d),
            pl.BlockSpec((1, 1, N_MOD * d), lambda i, j: (i, 0, 0)),
            _const_spec((1, d)),
            _layer_spec(w_in, layer),
            _layer_spec(conv_w, layer),
            _layer_spec(w_proj_conv, layer),
        ],
        out_specs=[tile(ATTN_WIDTH), tile(ATTN_WIDTH), tile(ATTN_WIDTH), tile(d), tile(d)],
        out_shape=[
            jax.ShapeDtypeStruct((b, s, ATTN_WIDTH), _bf16),
            jax.ShapeDtypeStruct((b, s, ATTN_WIDTH), _bf16),
            jax.ShapeDtypeStruct((b, s, ATTN_WIDTH), _bf16),
            jax.ShapeDtypeStruct((b, s, d), _f32),
            jax.ShapeDtypeStruct((b, s, d), _f32),
        ],
        scratch_shapes=[pltpu.VMEM((CONV_TAPS - 1, CONV_WIDTH), _f32)],
        compiler_params=pltpu.CompilerParams(
            dimension_semantics=("arbitrary", "arbitrary"), vmem_limit_bytes=VMEM_LIMIT),
        name="mixer_in",
    )(x, mod.reshape(b, 1, N_MOD * d), g_pre.reshape(1, d), w_in, conv_w, w_proj_conv)


def _attn_kernel(q_ref, k_ref, v_ref, o_ref, qh_ref, acc_ref, run_ref):
    tq = q_ref.shape[1]
    qi = pl.program_id(2)
    n_heads = qh_ref.shape[0]
    lanes_of = lambda h: slice((h // HEADS_PER_TILE) * LANES, (h // HEADS_PER_TILE + 1) * LANES)

    lane = lax.broadcasted_iota(jnp.int32, (tq, LANES), 1)
    own = [lane < HEAD_DIM, lane >= HEAD_DIM]
    for h in range(n_heads):
        q = q_ref[0, :, lanes_of(h)]
        qh_ref[h] = jnp.where(own[h % HEADS_PER_TILE], q, jnp.zeros_like(q))

    tri2 = (lax.broadcasted_iota(jnp.int32, (K_TILE, K_TILE), 0)
            > lax.broadcasted_iota(jnp.int32, (K_TILE, K_TILE), 1)).astype(_bf16)

    heads = range(n_heads)

    def rows_of(kb):
        return pl.ds(pl.multiple_of(kb * K_TILE, K_TILE), K_TILE)

    every_row = slice(0, tq)

    def logits(kb, h, rows=every_row):
        return lax.dot_general(qh_ref[h, rows, :], k_ref[0, rows_of(kb), lanes_of(h)],
                               (((1,), (1,)), ((), ())), preferred_element_type=_f32)

    def suffix_sums(z, diagonal):
        sp = jnp.maximum(jnp.log2(1.0 + jnp.exp2(jnp.minimum(z, Z_CLAMP))), z)
        log_beta = z - sp
        if diagonal:
            keep = (lax.broadcasted_iota(jnp.int32, z.shape, 1)
                    < lax.broadcasted_iota(jnp.int32, z.shape, 0))
            sp = jnp.where(keep, sp, 0.0)
            log_beta = jnp.where(keep, log_beta, MASKED_LOGIT)
        return log_beta, (_dot(sp.astype(_bf16), tri2), sp[:, 0:1])

    def accumulate(kb, h, log_beta, sums, first, rows=every_row):
        right_of, leftmost = sums
        block_sum = jnp.broadcast_to(right_of[:, 0:1] + leftmost, (log_beta.shape[0], LANES))
        values = v_ref[0, rows_of(kb), lanes_of(h)]
        if first:
            acc_ref[h, rows, :] = _dot(jnp.exp2(log_beta - right_of).astype(_bf16), values)
            run_ref[h, rows, :] = block_sum
        else:
            run = run_ref[h, rows, :]
            a = jnp.exp2(log_beta - right_of - jnp.concatenate([run] * (K_TILE // LANES), axis=1))
            acc_ref[h, rows, :] += _dot(a.astype(_bf16), values)
            run_ref[h, rows, :] = run + block_sum

    def run_min(rows):
        parts = [run_ref[h, rows, :] for h in heads]
        while len(parts) > 1:
            parts = [jnp.minimum(a, b) for a, b in zip(parts[0::2], parts[1::2])]
        m = parts[0]
        while m.shape[0] > SUBLANES:
            half = m.shape[0] // 2
            m = jnp.minimum(m[:half], m[half:])
        return jnp.min(m)

    def sweep(blocks):
        n = len(blocks)
        z = [None] * n
        suffix = [None] * n
        for phase in range(n + 2):
            for h in heads:
                if 0 <= phase - 2 < n:
                    accumulate(blocks[phase - 2], h, z[phase - 2][h], suffix[phase - 2][h],
                               first=(phase == 2))
                if 0 <= phase - 1 < n:
                    z[phase - 1][h], s = suffix_sums(z[phase - 1][h], diagonal=(phase == 1))
                    suffix[phase - 1] = (suffix[phase - 1] or []) + [s]
                if phase < n:
                    z[phase] = (z[phase] or []) + [logits(blocks[phase], h)]

    @pl.when(qi >= FUSED_BLOCKS - 1)
    def _():
        sweep([qi - i for i in range(FUSED_BLOCKS)])

    @pl.when(qi < FUSED_BLOCKS - 1)
    def _():
        sweep([qi])

    row_groups = [slice(r, r + tq // TAIL_ROW_GROUPS) for r in range(0, tq, tq // TAIL_ROW_GROUPS)]

    def run_mins():
        return tuple(run_min(rows) for rows in row_groups)

    def more(carry):
        kb, mins = carry
        return jnp.logical_and(kb >= 0, functools.reduce(jnp.minimum, mins) < UNDERFLOW_LOG2)

    def body(carry):
        kb, mins = carry
        for rows, group_min in zip(row_groups, mins):
            @pl.when(group_min < UNDERFLOW_LOG2)
            def _():
                z = [logits(kb, h, rows) for h in heads]
                staged = [suffix_sums(z[h], diagonal=False) for h in heads]
                for h in heads:
                    accumulate(kb, h, *staged[h], first=False, rows=rows)
        return kb - 1, run_mins()

    first_left = jnp.where(qi >= FUSED_BLOCKS - 1, qi - FUSED_BLOCKS, qi - 1)
    lax.while_loop(more, body, (first_left, run_mins()))

    for h in range(0, n_heads, HEADS_PER_TILE):
        o_ref[0, :, lanes_of(h)] = jnp.where(own[0], acc_ref[h], acc_ref[h + 1]).astype(o_ref.dtype)


def _attention(q, k, v):
    b, s, w = q.shape
    width = ATTN_LANE_TILES * LANES
    n_heads = ATTN_LANE_TILES * HEADS_PER_TILE
    assert Q_TILE == K_TILE and s % Q_TILE == 0 and w % width == 0
    seq = pl.BlockSpec((1, s, width), lambda i, h, j: (i, 0, h))
    tile = pl.BlockSpec((1, Q_TILE, width), lambda i, h, j: (i, j, h))
    return pl.pallas_call(
        _attn_kernel,
        grid=(b, w // width, s // Q_TILE),
        in_specs=[tile, seq, seq],
        out_specs=tile,
        out_shape=jax.ShapeDtypeStruct((b, s, w), _bf16),
        scratch_shapes=[
            pltpu.VMEM((n_heads, Q_TILE, LANES), _bf16),
            pltpu.VMEM((n_heads, Q_TILE, LANES), _f32),
            pltpu.VMEM((n_heads, Q_TILE, LANES), _f32),
        ],
        compiler_params=pltpu.CompilerParams(
            dimension_semantics=("arbitrary", "arbitrary", "arbitrary"),
            vmem_limit_bytes=VMEM_LIMIT),
        name="attention",
    )(q, k, v)


def _mixer_out_kernel(x_ref, o_ref, gc_ref, sgb_ref, mod_ref, gains_ref,
                      wpa_ref, wout_ref, w1_ref, w2_ref, out_ref):
    d = x_ref.shape[-1]
    mod = lambda i: mod_ref[0, :, i * d:(i + 1) * d]
    gate1, shift2, scale2, gate2 = mod(2), mod(3), mod(4), mod(5)
    g_post_mix = gains_ref[0:1, :]
    g_pre_mlp = gains_ref[1:2, :]
    g_post_mlp = gains_ref[2:3, :]

    tm = x_ref.shape[1]
    subs = [pl.ds(r, tm // ROW_SUBTILES) for r in range(0, tm, tm // ROW_SUBTILES)]

    mix = []
    for rows in subs:
        y_attn = _dot(o_ref[0, rows, :], wpa_ref[...])
        merged = gc_ref[0, rows, :] + sgb_ref[0, rows, :] * y_attn
        mix.append(_dot(merged, wout_ref[...]))
    x1, h2 = [], []
    for i, rows in enumerate(subs):
        x1.append(x_ref[0, rows, :] + gate1 * _rms(mix[i], g_post_mix))
        h2.append(_rms(x1[i], g_pre_mlp) * (1.0 + scale2) + shift2)
    for i, rows in enumerate(subs):
        ff = jnp.zeros_like(x1[i])
        for j in range(0, w1_ref.shape[1], FF_CHUNK):
            hid = jnp.maximum(_dot(h2[i], w1_ref[:, j:j + FF_CHUNK]), 0.0)
            ff = ff + _dot((hid * hid).astype(_bf16), w2_ref[j:j + FF_CHUNK, :])
        out_ref[0, rows, :] = x1[i] + gate2 * _rms(ff, g_post_mlp)


def _mixer_out(layer, x, o, gc, sgb, mod, gains, w_proj_attn, w_out, w1, w2):
    b, s, d = x.shape
    tm = SEQ_TILE
    tile = lambda width: pl.BlockSpec((1, tm, width), lambda i, j: (i, j, 0))
    return pl.pallas_call(
        _mixer_out_kernel,
        grid=(b, s // tm),
        in_specs=[
            tile(d), tile(ATTN_WIDTH), tile(d), tile(d),
            pl.BlockSpec((1, 1, N_MOD * d), lambda i, j: (i, 0, 0)),
            _const_spec(gains.shape),
            _layer_spec(w_proj_attn, layer),
            _layer_spec(w_out, layer),
            _layer_spec(w1, layer),
            _layer_spec(w2, layer),
        ],
        out_specs=tile(d),
        out_shape=jax.ShapeDtypeStruct((b, s, d), _f32),
        compiler_params=pltpu.CompilerParams(
            dimension_semantics=("arbitrary", "arbitrary"), vmem_limit_bytes=VMEM_LIMIT),
        name="mixer_out",
    )(x, o, gc, sgb, mod.reshape(b, 1, N_MOD * d), gains, w_proj_attn, w_out, w1, w2)


def kernel(x, c, w_ada, b_ada, g_pre_mix, g_post_mix, g_pre_mlp, g_post_mlp, w_in, conv_w,
           w_proj_conv, w_proj_attn, w_out, w_mlp_in, w_mlp_out):
    depth = w_ada.shape[0]
    mod = _ada(c, w_ada, b_ada)
    w_mlp_out = w_mlp_out.astype(_bf16)
    for l in range(depth):
        q, k, v, gc, sgb = _mixer_in(l, x, mod[l], g_pre_mix[l], w_in, conv_w, w_proj_conv)
        o = _attention(q, k, v)
        gains = jnp.stack([g_post_mix[l], g_pre_mlp[l], g_post_mlp[l]])
        x = _mixer_out(l, x, o, gc, sgb, mod[l], gains, w_proj_attn, w_out, w_mlp_in, w_mlp_out)
    return x
```

```python
import functools
import math

import jax
import jax.numpy as jnp
from jax import lax
from jax.experimental import pallas as pl
from jax.experimental.pallas import tpu as pltpu

CONV_WIDTH = 512
CONV_TAPS = 3
ATTN_WIDTH = 512
HEAD_DIM = 64
N_MOD = 6
EPS = 1e-6

LANES = 128
SUBLANES = 8
HEADS_PER_TILE = LANES // HEAD_DIM
ATTN_LANE_TILES = 4
SEQ_TILE = 512
Q_TILE = 256
K_TILE = 256
ADA_COLS = 2048
FF_CHUNK = 1024
ROW_SUBTILES = 2
VMEM_LIMIT = 56 * 1024 * 1024
LOG2E = math.log2(math.e)
UNDERFLOW_LOG2 = 152.0
FUSED_BLOCKS = 3
HEAD_SKEW = 1
TAIL_ROW_GROUPS = 2
MASKED_LOGIT = -1e30
Z_CLAMP = 120.0

_f32 = jnp.float32
_bf16 = jnp.bfloat16


def _dot(a, b):
    return jnp.dot(a, b, preferred_element_type=_f32)


def _rms(x, gain):
    return x * lax.rsqrt(jnp.mean(x * x, axis=-1, keepdims=True) + EPS) * gain


def _layer_spec(stacked, layer):
    rest = stacked.shape[1:]
    return pl.BlockSpec((None,) + rest, lambda *_: (layer,) + (0,) * len(rest),
                        pipeline_mode=pl.Buffered(1))


def _const_spec(shape):
    return pl.BlockSpec(shape, lambda *_: (0,) * len(shape), pipeline_mode=pl.Buffered(1))


def _ada_kernel(ct_ref, w_ref, b_ref, o_ref):
    w = w_ref[0]
    rows = [jnp.sum(w * ct_ref[:, i:i + 1], axis=0, keepdims=True) for i in range(ct_ref.shape[1])]
    o_ref[0] = jnp.concatenate(rows, axis=0) + b_ref[0, 0]


def _ada(c, w_ada, b_ada):
    depth, d, width = w_ada.shape
    b = c.shape[0]
    cols = ADA_COLS
    return pl.pallas_call(
        _ada_kernel,
        grid=(depth, width // cols),
        in_specs=[
            pl.BlockSpec((d, b), lambda l, j: (0, 0)),
            pl.BlockSpec((1, d, cols), lambda l, j: (l, 0, j)),
            pl.BlockSpec((1, 1, 1, cols), lambda l, j: (l, j, 0, 0)),
        ],
        out_specs=pl.BlockSpec((1, b, cols), lambda l, j: (l, 0, j)),
        out_shape=jax.ShapeDtypeStruct((depth, b, width), _f32),
        compiler_params=pltpu.CompilerParams(
            dimension_semantics=("arbitrary", "arbitrary"), vmem_limit_bytes=VMEM_LIMIT),
        name="ada",
    )(c.T, w_ada, b_ada.reshape(depth, width // cols, 1, cols))


def _mixer_in_kernel(x_ref, mod_ref, g_ref, w_in_ref, cw_ref, wpc_ref,
                     q_ref, k_ref, v_ref, gc_ref, sgb_ref, carry_ref):
    d = x_ref.shape[-1]
    tm = x_ref.shape[1]
    cw = CONV_WIDTH

    @pl.when(pl.program_id(1) == 0)
    def _():
        carry_ref[...] = jnp.zeros_like(carry_ref)

    shift = mod_ref[0, :, 0:d]
    scale = mod_ref[0, :, d:2 * d]
    h = _rms(x_ref[0], g_ref[...]) * (1.0 + scale) + shift

    bg = _dot(h, w_in_ref[:, 0:cw])
    cg = _dot(h, w_in_ref[:, cw:2 * cw])
    u = _dot(h, w_in_ref[:, 2 * cw:3 * cw])

    o = 3 * cw
    aw = ATTN_WIDTH
    q_ref[0] = (_dot(h, w_in_ref[:, o:o + aw]) * (LOG2E / math.sqrt(HEAD_DIM))).astype(_bf16)
    k_ref[0] = _dot(h, w_in_ref[:, o + aw:o + 2 * aw]).astype(_bf16)
    v_ref[0] = _dot(h, w_in_ref[:, o + 2 * aw:o + 3 * aw]).astype(_bf16)
    o += 3 * aw
    gb = _dot(h, w_in_ref[:, o + d:o + 2 * d])
    sgb_ref[0] = jax.nn.sigmoid(gb)
    ga = _dot(h, w_in_ref[:, o:o + d])

    vv = cg * u
    row = lax.broadcasted_iota(jnp.int32, vv.shape, 0)
    prev1 = carry_ref[1:2, :]
    prev2 = carry_ref[0:1, :]
    v1 = jnp.where(row == 0, prev1, pltpu.roll(vv, 1, axis=0))
    v2 = jnp.where(row == 0, prev2, jnp.where(row == 1, prev1, pltpu.roll(vv, 2, axis=0)))
    carry_ref[...] = vv[tm - 2:tm, :]
    conv = cw_ref[0:1, :] * v2 + cw_ref[1:2, :] * v1 + cw_ref[2:3, :] * vv
    y_conv = _dot(bg * conv, wpc_ref[...])
    gc_ref[0] = jax.nn.sigmoid(ga) * y_conv


def _mixer_in(layer, x, mod, g_pre, w_in, conv_w, w_proj_conv):
    b, s, d = x.shape
    tm = SEQ_TILE
    tile = lambda width: pl.BlockSpec((1, tm, width), lambda i, j: (i, j, 0))
    return pl.pallas_call(
        _mixer_in_kernel,
        grid=(b, s // tm),
        in_specs=[
            tile(d),
            pl.BlockSpec((1, 1, N_MOD * d), lambda i, j: (i, 0, 0)),
            _const_spec((1, d)),
            _layer_spec(w_in, layer),
            _layer_spec(conv_w, layer),
            _layer_spec(w_proj_conv, layer),
        ],
        out_specs=[tile(ATTN_WIDTH), tile(ATTN_WIDTH), tile(ATTN_WIDTH), tile(d), tile(d)],
        out_shape=[
            jax.ShapeDtypeStruct((b, s, ATTN_WIDTH), _bf16),
            jax.ShapeDtypeStruct((b, s, ATTN_WIDTH), _bf16),
            jax.ShapeDtypeStruct((b, s, ATTN_WIDTH), _bf16),
            jax.ShapeDtypeStruct((b, s, d), _f32),
            jax.ShapeDtypeStruct((b, s, d), _f32),
        ],
        scratch_shapes=[pltpu.VMEM((CONV_TAPS - 1, CONV_WIDTH), _f32)],
        compiler_params=pltpu.CompilerParams(
            dimension_semantics=("arbitrary", "arbitrary"), vmem_limit_bytes=VMEM_LIMIT),
        name="mixer_in",
    )(x, mod.reshape(b, 1, N_MOD * d), g_pre.reshape(1, d), w_in, conv_w, w_proj_conv)


def _attn_kernel(q_ref, k_ref, v_ref, o_ref, qh_ref, acc_ref, run_ref):
    tq = q_ref.shape[1]
    qi = pl.program_id(2)
    n_heads = qh_ref.shape[0]
    lanes_of = lambda h: slice((h // HEADS_PER_TILE) * LANES, (h // HEADS_PER_TILE + 1) * LANES)

    lane = lax.broadcasted_iota(jnp.int32, (tq, LANES), 1)
    own = [lane < HEAD_DIM, lane >= HEAD_DIM]
    for h in range(n_heads):
        q = q_ref[0, :, lanes_of(h)]
        qh_ref[h] = jnp.where(own[h % HEADS_PER_TILE], q, jnp.zeros_like(q))

    tri2 = (lax.broadcasted_iota(jnp.int32, (K_TILE, K_TILE), 0)
            > lax.broadcasted_iota(jnp.int32, (K_TILE, K_TILE), 1)).astype(_bf16)

    heads = range(n_heads)

    def rows_of(kb):
        return pl.ds(pl.multiple_of(kb * K_TILE, K_TILE), K_TILE)

    every_row = slice(0, tq)

    def logits(kb, h, rows=every_row):
        return lax.dot_general(qh_ref[h, rows, :], k_ref[0, rows_of(kb), lanes_of(h)],
                               (((1,), (1,)), ((), ())), preferred_element_type=_f32)

    def suffix_sums(z, diagonal):
        sp = jnp.maximum(jnp.log2(1.0 + jnp.exp2(jnp.minimum(z, Z_CLAMP))), z)
        log_beta = z - sp
        if diagonal:
            keep = (lax.broadcasted_iota(jnp.int32, z.shape, 1)
                    < lax.broadcasted_iota(jnp.int32, z.shape, 0))
            sp = jnp.where(keep, sp, 0.0)
            log_beta = jnp.where(keep, log_beta, MASKED_LOGIT)
        return log_beta, (_dot(sp.astype(_bf16), tri2), sp[:, 0:1])

    def accumulate(kb, h, log_beta, sums, first, rows=every_row):
        right_of, leftmost = sums
        block_sum = jnp.broadcast_to(right_of[:, 0:1] + leftmost, (log_beta.shape[0], LANES))
        values = v_ref[0, rows_of(kb), lanes_of(h)]
        if first:
            acc_ref[h, rows, :] = _dot(jnp.exp2(log_beta - right_of).astype(_bf16), values)
            run_ref[h, rows, :] = block_sum
        else:
            run = run_ref[h, rows, :]
            a = jnp.exp2(log_beta - right_of - jnp.concatenate([run] * (K_TILE // LANES), axis=1))
            acc_ref[h, rows, :] += _dot(a.astype(_bf16), values)
            run_ref[h, rows, :] = run + block_sum

    def run_min(rows):
        parts = [run_ref[h, rows, :] for h in heads]
        while len(parts) > 1:
            parts = [jnp.minimum(a, b) for a, b in zip(parts[0::2], parts[1::2])]
        m = parts[0]
        while m.shape[0] > SUBLANES:
            half = m.shape[0] // 2
            m = jnp.minimum(m[:half], m[half:])
        return jnp.min(m)

    def sweep(blocks, tile_start, rows=every_row):
        n = len(blocks)
        staged = [[None] * n_heads for _ in range(n)]
        for phase in range(n + 1):
            z = [None] * n_heads
            for slot in range(n_heads + HEAD_SKEW):
                if phase >= 1 and slot < n_heads:
                    accumulate(blocks[phase - 1], slot, *staged[phase - 1][slot],
                               first=(tile_start and phase == 1), rows=rows)
                if phase < n and slot < n_heads:
                    z[slot] = logits(blocks[phase], slot, rows)
                if phase < n and 0 <= slot - HEAD_SKEW < n_heads:
                    h = slot - HEAD_SKEW
                    staged[phase][h] = suffix_sums(z[h], diagonal=(tile_start and phase == 0))

    @pl.when(qi >= FUSED_BLOCKS - 1)
    def _():
        sweep([qi - i for i in range(FUSED_BLOCKS)], tile_start=True)

    @pl.when(qi < FUSED_BLOCKS - 1)
    def _():
        sweep([qi], tile_start=True)

    row_groups = [slice(r, r + tq // TAIL_ROW_GROUPS) for r in range(0, tq, tq // TAIL_ROW_GROUPS)]

    def run_mins():
        return tuple(run_min(rows) for rows in row_groups)

    def more(carry):
        kb, mins = carry
        return jnp.logical_and(kb >= 0, functools.reduce(jnp.minimum, mins) < UNDERFLOW_LOG2)

    def body(carry):
        kb, mins = carry
        for rows, group_min in zip(row_groups, mins):
            @pl.when(group_min < UNDERFLOW_LOG2)
            def _():
                sweep([kb], tile_start=False, rows=rows)
        return kb - 1, run_mins()

    first_left = jnp.where(qi >= FUSED_BLOCKS - 1, qi - FUSED_BLOCKS, qi - 1)
    lax.while_loop(more, body, (first_left, run_mins()))

    for h in range(0, n_heads, HEADS_PER_TILE):
        o_ref[0, :, lanes_of(h)] = jnp.where(own[0], acc_ref[h], acc_ref[h + 1]).astype(o_ref.dtype)


def _attention(q, k, v):
    b, s, w = q.shape
    width = ATTN_LANE_TILES * LANES
    n_heads = ATTN_LANE_TILES * HEADS_PER_TILE
    assert Q_TILE == K_TILE and s % Q_TILE == 0 and w % width == 0
    seq = pl.BlockSpec((1, s, width), lambda i, h, j: (i, 0, h))
    tile = pl.BlockSpec((1, Q_TILE, width), lambda i, h, j: (i, j, h))
    return pl.pallas_call(
        _attn_kernel,
        grid=(b, w // width, s // Q_TILE),
        in_specs=[tile, seq, seq],
        out_specs=tile,
        out_shape=jax.ShapeDtypeStruct((b, s, w), _bf16),
        scratch_shapes=[
            pltpu.VMEM((n_heads, Q_TILE, LANES), _bf16),
            pltpu.VMEM((n_heads, Q_TILE, LANES), _f32),
            pltpu.VMEM((n_heads, Q_TILE, LANES), _f32),
        ],
        compiler_params=pltpu.CompilerParams(
            dimension_semantics=("arbitrary", "arbitrary", "arbitrary"),
            vmem_limit_bytes=VMEM_LIMIT),
        name="attention",
    )(q, k, v)


def _mixer_out_kernel(x_ref, o_ref, gc_ref, sgb_ref, mod_ref, gains_ref,
                      wpa_ref, wout_ref, w1_ref, w2_ref, out_ref):
    d = x_ref.shape[-1]
    mod = lambda i: mod_ref[0, :, i * d:(i + 1) * d]
    gate1, shift2, scale2, gate2 = mod(2), mod(3), mod(4), mod(5)
    g_post_mix = gains_ref[0:1, :]
    g_pre_mlp = gains_ref[1:2, :]
    g_post_mlp = gains_ref[2:3, :]

    tm = x_ref.shape[1]
    subs = [pl.ds(r, tm // ROW_SUBTILES) for r in range(0, tm, tm // ROW_SUBTILES)]

    mix = []
    for rows in subs:
        y_attn = _dot(o_ref[0, rows, :], wpa_ref[...])
        merged = gc_ref[0, rows, :] + sgb_ref[0, rows, :] * y_attn
        mix.append(_dot(merged, wout_ref[...]))
    x1, h2 = [], []
    for i, rows in enumerate(subs):
        x1.append(x_ref[0, rows, :] + gate1 * _rms(mix[i], g_post_mix))
        h2.append(_rms(x1[i], g_pre_mlp) * (1.0 + scale2) + shift2)
    for i, rows in enumerate(subs):
        ff = jnp.zeros_like(x1[i])
        for j in range(0, w1_ref.shape[1], FF_CHUNK):
            hid = jnp.maximum(_dot(h2[i], w1_ref[:, j:j + FF_CHUNK]), 0.0)
            ff = ff + _dot((hid * hid).astype(_bf16), w2_ref[j:j + FF_CHUNK, :])
        out_ref[0, rows, :] = x1[i] + gate2 * _rms(ff, g_post_mlp)


def _mixer_out(layer, x, o, gc, sgb, mod, gains, w_proj_attn, w_out, w1, w2):
    b, s, d = x.shape
    tm = SEQ_TILE
    tile = lambda width: pl.BlockSpec((1, tm, width), lambda i, j: (i, j, 0))
    return pl.pallas_call(
        _mixer_out_kernel,
        grid=(b, s // tm),
        in_specs=[
            tile(d), tile(ATTN_WIDTH), tile(d), tile(d),
            pl.BlockSpec((1, 1, N_MOD * d), lambda i, j: (i, 0, 0)),
            _const_spec(gains.shape),
            _layer_spec(w_proj_attn, layer),
            _layer_spec(w_out, layer),
            _layer_spec(w1, layer),
            _layer_spec(w2, layer),
        ],
        out_specs=tile(d),
        out_shape=jax.ShapeDtypeStruct((b, s, d), _f32),
        compiler_params=pltpu.CompilerParams(
            dimension_semantics=("arbitrary", "arbitrary"), vmem_limit_bytes=VMEM_LIMIT),
        name="mixer_out",
    )(x, o, gc, sgb, mod.reshape(b, 1, N_MOD * d), gains, w_proj_attn, w_out, w1, w2)


def kernel(x, c, w_ada, b_ada, g_pre_mix, g_post_mix, g_pre_mlp, g_post_mlp, w_in, conv_w,
           w_proj_conv, w_proj_attn, w_out, w_mlp_in, w_mlp_out):
    depth = w_ada.shape[0]
    mod = _ada(c, w_ada, b_ada)
    w_mlp_out = w_mlp_out.astype(_bf16)
    for l in range(depth):
        q, k, v, gc, sgb = _mixer_in(l, x, mod[l], g_pre_mix[l], w_in, conv_w, w_proj_conv)
        o = _attention(q, k, v)
        gains = jnp.stack([g_post_mix[l], g_pre_mlp[l], g_post_mlp[l]])
        x = _mixer_out(l, x, o, gc, sgb, mod[l], gains, w_proj_attn, w_out, w_mlp_in, w_mlp_out)
    return x
```

```python
import functools
import math

import jax
import jax.numpy as jnp
from jax import lax
from jax.experimental import pallas as pl
from jax.experimental.pallas import tpu as pltpu

CONV_WIDTH = 512
CONV_TAPS = 3
ATTN_WIDTH = 512
HEAD_DIM = 64
N_MOD = 6
EPS = 1e-6

LANES = 128
SUBLANES = 8
HEADS_PER_TILE = LANES // HEAD_DIM
ATTN_LANE_TILES = 4
SEQ_TILE = 512
Q_TILE = 256
K_TILE = 256
ADA_COLS = 2048
FF_CHUNK = 1024
ROW_SUBTILES = 2
VMEM_LIMIT = 56 * 1024 * 1024
LOG2E = math.log2(math.e)
UNDERFLOW_LOG2 = 152.0
FUSED_BLOCKS = 3
HEAD_SKEW = 1
VALUE_SKEW = 3
TAIL_ROW_GROUPS = 2
MASKED_LOGIT = -1e30
Z_CLAMP = 120.0

_f32 = jnp.float32
_bf16 = jnp.bfloat16


def _dot(a, b):
    return jnp.dot(a, b, preferred_element_type=_f32)


def _rms(x, gain):
    return x * lax.rsqrt(jnp.mean(x * x, axis=-1, keepdims=True) + EPS) * gain


def _layer_spec(stacked, layer):
    rest = stacked.shape[1:]
    return pl.BlockSpec((None,) + rest, lambda *_: (layer,) + (0,) * len(rest),
                        pipeline_mode=pl.Buffered(1))


def _const_spec(shape):
    return pl.BlockSpec(shape, lambda *_: (0,) * len(shape), pipeline_mode=pl.Buffered(1))


def _ada_kernel(ct_ref, w_ref, b_ref, o_ref):
    w = w_ref[0]
    rows = [jnp.sum(w * ct_ref[:, i:i + 1], axis=0, keepdims=True) for i in range(ct_ref.shape[1])]
    o_ref[0] = jnp.concatenate(rows, axis=0) + b_ref[0, 0]


def _ada(c, w_ada, b_ada):
    depth, d, width = w_ada.shape
    b = c.shape[0]
    cols = ADA_COLS
    return pl.pallas_call(
        _ada_kernel,
        grid=(depth, width // cols),
        in_specs=[
            pl.BlockSpec((d, b), lambda l, j: (0, 0)),
            pl.BlockSpec((1, d, cols), lambda l, j: (l, 0, j)),
            pl.BlockSpec((1, 1, 1, cols), lambda l, j: (l, j, 0, 0)),
        ],
        out_specs=pl.BlockSpec((1, b, cols), lambda l, j: (l, 0, j)),
        out_shape=jax.ShapeDtypeStruct((depth, b, width), _f32),
        compiler_params=pltpu.CompilerParams(
            dimension_semantics=("arbitrary", "arbitrary"), vmem_limit_bytes=VMEM_LIMIT),
        name="ada",
    )(c.T, w_ada, b_ada.reshape(depth, width // cols, 1, cols))


def _mixer_in_kernel(x_ref, mod_ref, g_ref, w_in_ref, cw_ref, wpc_ref,
                     q_ref, k_ref, v_ref, gc_ref, sgb_ref, carry_ref):
    d = x_ref.shape[-1]
    tm = x_ref.shape[1]
    cw = CONV_WIDTH

    @pl.when(pl.program_id(1) == 0)
    def _():
        carry_ref[...] = jnp.zeros_like(carry_ref)

    shift = mod_ref[0, :, 0:d]
    scale = mod_ref[0, :, d:2 * d]
    h = _rms(x_ref[0], g_ref[...]) * (1.0 + scale) + shift

    bg = _dot(h, w_in_ref[:, 0:cw])
    cg = _dot(h, w_in_ref[:, cw:2 * cw])
    u = _dot(h, w_in_ref[:, 2 * cw:3 * cw])

    o = 3 * cw
    aw = ATTN_WIDTH
    q_ref[0] = (_dot(h, w_in_ref[:, o:o + aw]) * (LOG2E / math.sqrt(HEAD_DIM))).astype(_bf16)
    k_ref[0] = _dot(h, w_in_ref[:, o + aw:o + 2 * aw]).astype(_bf16)
    v_ref[0] = _dot(h, w_in_ref[:, o + 2 * aw:o + 3 * aw]).astype(_bf16)
    o += 3 * aw
    gb = _dot(h, w_in_ref[:, o + d:o + 2 * d])
    sgb_ref[0] = jax.nn.sigmoid(gb)
    ga = _dot(h, w_in_ref[:, o:o + d])

    vv = cg * u
    row = lax.broadcasted_iota(jnp.int32, vv.shape, 0)
    prev1 = carry_ref[1:2, :]
    prev2 = carry_ref[0:1, :]
    v1 = jnp.where(row == 0, prev1, pltpu.roll(vv, 1, axis=0))
    v2 = jnp.where(row == 0, prev2, jnp.where(row == 1, prev1, pltpu.roll(vv, 2, axis=0)))
    carry_ref[...] = vv[tm - 2:tm, :]
    conv = cw_ref[0:1, :] * v2 + cw_ref[1:2, :] * v1 + cw_ref[2:3, :] * vv
    y_conv = _dot(bg * conv, wpc_ref[...])
    gc_ref[0] = jax.nn.sigmoid(ga) * y_conv


def _mixer_in(layer, x, mod, g_pre, w_in, conv_w, w_proj_conv):
    b, s, d = x.shape
    tm = SEQ_TILE
    tile = lambda width: pl.BlockSpec((1, tm, width), lambda i, j: (i, j, 0))
    return pl.pallas_call(
        _mixer_in_kernel,
        grid=(b, s // tm),
        in_specs=[
            tile(d),
            pl.BlockSpec((1, 1, N_MOD * d), lambda i, j: (i, 0, 0)),
            _const_spec((1, d)),
            _layer_spec(w_in, layer),
            _layer_spec(conv_w, layer),
            _layer_spec(w_proj_conv, layer),
        ],
        out_specs=[tile(ATTN_WIDTH), tile(ATTN_WIDTH), tile(ATTN_WIDTH), tile(d), tile(d)],
        out_shape=[
            jax.ShapeDtypeStruct((b, s, ATTN_WIDTH), _bf16),
            jax.ShapeDtypeStruct((b, s, ATTN_WIDTH), _bf16),
            jax.ShapeDtypeStruct((b, s, ATTN_WIDTH), _bf16),
            jax.ShapeDtypeStruct((b, s, d), _f32),
            jax.ShapeDtypeStruct((b, s, d), _f32),
        ],
        scratch_shapes=[pltpu.VMEM((CONV_TAPS - 1, CONV_WIDTH), _f32)],
        compiler_params=pltpu.CompilerParams(
            dimension_semantics=("arbitrary", "arbitrary"), vmem_limit_bytes=VMEM_LIMIT),
        name="mixer_in",
    )(x, mod.reshape(b, 1, N_MOD * d), g_pre.reshape(1, d), w_in, conv_w, w_proj_conv)


def _attn_kernel(q_ref, k_ref, v_ref, o_ref, qh_ref, acc_ref, run_ref):
    tq = q_ref.shape[1]
    qi = pl.program_id(2)
    n_heads = qh_ref.shape[0]
    lanes_of = lambda h: slice((h // HEADS_PER_TILE) * LANES, (h // HEADS_PER_TILE + 1) * LANES)

    lane = lax.broadcasted_iota(jnp.int32, (tq, LANES), 1)
    own = [lane < HEAD_DIM, lane >= HEAD_DIM]
    for h in range(n_heads):
        q = q_ref[0, :, lanes_of(h)]
        qh_ref[h] = jnp.where(own[h % HEADS_PER_TILE], q, jnp.zeros_like(q))

    tri2 = (lax.broadcasted_iota(jnp.int32, (K_TILE, K_TILE), 0)
            > lax.broadcasted_iota(jnp.int32, (K_TILE, K_TILE), 1)).astype(_bf16)

    heads = range(n_heads)

    def rows_of(kb):
        return pl.ds(pl.multiple_of(kb * K_TILE, K_TILE), K_TILE)

    every_row = slice(0, tq)

    def logits(kb, h, rows=every_row):
        return lax.dot_general(qh_ref[h, rows, :], k_ref[0, rows_of(kb), lanes_of(h)],
                               (((1,), (1,)), ((), ())), preferred_element_type=_f32)

    def suffix_sums(z, diagonal):
        sp = jnp.maximum(jnp.log2(1.0 + jnp.exp2(jnp.minimum(z, Z_CLAMP))), z)
        log_beta = z - sp
        if diagonal:
            keep = (lax.broadcasted_iota(jnp.int32, z.shape, 1)
                    < lax.broadcasted_iota(jnp.int32, z.shape, 0))
            sp = jnp.where(keep, sp, 0.0)
            log_beta = jnp.where(keep, log_beta, MASKED_LOGIT)
        return log_beta, (_dot(sp.astype(_bf16), tri2), sp[:, 0:1])

    def accumulate(kb, h, log_beta, sums, first, rows=every_row):
        right_of, leftmost = sums
        block_sum = jnp.broadcast_to(right_of[:, 0:1] + leftmost, (log_beta.shape[0], LANES))
        values = v_ref[0, rows_of(kb), lanes_of(h)]
        if first:
            acc_ref[h, rows, :] = _dot(jnp.exp2(log_beta - right_of).astype(_bf16), values)
            run_ref[h, rows, :] = block_sum
        else:
            run = run_ref[h, rows, :]
            a = jnp.exp2(log_beta - right_of - jnp.concatenate([run] * (K_TILE // LANES), axis=1))
            acc_ref[h, rows, :] += _dot(a.astype(_bf16), values)
            run_ref[h, rows, :] = run + block_sum

    def run_min(rows):
        parts = [run_ref[h, rows, :] for h in heads]
        while len(parts) > 1:
            parts = [jnp.minimum(a, b) for a, b in zip(parts[0::2], parts[1::2])]
        m = parts[0]
        while m.shape[0] > SUBLANES:
            half = m.shape[0] // 2
            m = jnp.minimum(m[:half], m[half:])
        return jnp.min(m)

    def sweep(blocks, tile_start, value_skew, rows=every_row):
        work = [(i, h) for i in range(len(blocks)) for h in heads]
        z, staged = {}, {}
        for t in range(len(work) + value_skew):
            if 0 <= t - value_skew:
                i, h = work[t - value_skew]
                accumulate(blocks[i], h, *staged.pop((i, h)), first=(tile_start and i == 0), rows=rows)
            if t < len(work):
                i, h = work[t]
                z[i, h] = logits(blocks[i], h, rows)
            if 0 <= t - HEAD_SKEW < len(work):
                i, h = work[t - HEAD_SKEW]
                staged[i, h] = suffix_sums(z.pop((i, h)), diagonal=(tile_start and i == 0))

    @pl.when(qi >= FUSED_BLOCKS - 1)
    def _():
        sweep([qi - i for i in range(FUSED_BLOCKS)], tile_start=True, value_skew=VALUE_SKEW)

    @pl.when(qi < FUSED_BLOCKS - 1)
    def _():
        sweep([qi], tile_start=True, value_skew=VALUE_SKEW)

    row_groups = [slice(r, r + tq // TAIL_ROW_GROUPS) for r in range(0, tq, tq // TAIL_ROW_GROUPS)]

    def run_mins():
        return tuple(run_min(rows) for rows in row_groups)

    def more(carry):
        kb, mins = carry
        return jnp.logical_and(kb >= 0, functools.reduce(jnp.minimum, mins) < UNDERFLOW_LOG2)

    def body(carry):
        kb, mins = carry
        for rows, group_min in zip(row_groups, mins):
            @pl.when(group_min < UNDERFLOW_LOG2)
            def _():
                sweep([kb], tile_start=False, value_skew=n_heads, rows=rows)
        return kb - 1, run_mins()

    first_left = jnp.where(qi >= FUSED_BLOCKS - 1, qi - FUSED_BLOCKS, qi - 1)
    lax.while_loop(more, body, (first_left, run_mins()))

    for h in range(0, n_heads, HEADS_PER_TILE):
        o_ref[0, :, lanes_of(h)] = jnp.where(own[0], acc_ref[h], acc_ref[h + 1]).astype(o_ref.dtype)


def _attention(q, k, v):
    b, s, w = q.shape
    width = ATTN_LANE_TILES * LANES
    n_heads = ATTN_LANE_TILES * HEADS_PER_TILE
    assert Q_TILE == K_TILE and s % Q_TILE == 0 and w % width == 0
    seq = pl.BlockSpec((1, s, width), lambda i, h, j: (i, 0, h))
    tile = pl.BlockSpec((1, Q_TILE, width), lambda i, h, j: (i, j, h))
    return pl.pallas_call(
        _attn_kernel,
        grid=(b, w // width, s // Q_TILE),
        in_specs=[tile, seq, seq],
        out_specs=tile,
        out_shape=jax.ShapeDtypeStruct((b, s, w), _bf16),
        scratch_shapes=[
            pltpu.VMEM((n_heads, Q_TILE, LANES), _bf16),
            pltpu.VMEM((n_heads, Q_TILE, LANES), _f32),
            pltpu.VMEM((n_heads, Q_TILE, LANES), _f32),
        ],
        compiler_params=pltpu.CompilerParams(
            dimension_semantics=("arbitrary", "arbitrary", "arbitrary"),
            vmem_limit_bytes=VMEM_LIMIT),
        name="attention",
    )(q, k, v)


def _mixer_out_kernel(x_ref, o_ref, gc_ref, sgb_ref, mod_ref, gains_ref,
                      wpa_ref, wout_ref, w1_ref, w2_ref, out_ref):
    d = x_ref.shape[-1]
    mod = lambda i: mod_ref[0, :, i * d:(i + 1) * d]
    gate1, shift2, scale2, gate2 = mod(2), mod(3), mod(4), mod(5)
    g_post_mix = gains_ref[0:1, :]
    g_pre_mlp = gains_ref[1:2, :]
    g_post_mlp = gains_ref[2:3, :]

    tm = x_ref.shape[1]
    subs = [pl.ds(r, tm // ROW_SUBTILES) for r in range(0, tm, tm // ROW_SUBTILES)]

    mix = []
    for rows in subs:
        y_attn = _dot(o_ref[0, rows, :], wpa_ref[...])
        merged = gc_ref[0, rows, :] + sgb_ref[0, rows, :] * y_attn
        mix.append(_dot(merged, wout_ref[...]))
    x1, h2 = [], []
    for i, rows in enumerate(subs):
        x1.append(x_ref[0, rows, :] + gate1 * _rms(mix[i], g_post_mix))
        h2.append(_rms(x1[i], g_pre_mlp) * (1.0 + scale2) + shift2)
    for i, rows in enumerate(subs):
        ff = jnp.zeros_like(x1[i])
        for j in range(0, w1_ref.shape[1], FF_CHUNK):
            hid = jnp.maximum(_dot(h2[i], w1_ref[:, j:j + FF_CHUNK]), 0.0)
            ff = ff + _dot((hid * hid).astype(_bf16), w2_ref[j:j + FF_CHUNK, :])
        out_ref[0, rows, :] = x1[i] + gate2 * _rms(ff, g_post_mlp)


def _mixer_out(layer, x, o, gc, sgb, mod, gains, w_proj_attn, w_out, w1, w2):
    b, s, d = x.shape
    tm = SEQ_TILE
    tile = lambda width: pl.BlockSpec((1, tm, width), lambda i, j: (i, j, 0))
    return pl.pallas_call(
        _mixer_out_kernel,
        grid=(b, s // tm),
        in_specs=[
            tile(d), tile(ATTN_WIDTH), tile(d), tile(d),
            pl.BlockSpec((1, 1, N_MOD * d), lambda i, j: (i, 0, 0)),
            _const_spec(gains.shape),
            _layer_spec(w_proj_attn, layer),
            _layer_spec(w_out, layer),
            _layer_spec(w1, layer),
            _layer_spec(w2, layer),
        ],
        out_specs=tile(d),
        out_shape=jax.ShapeDtypeStruct((b, s, d), _f32),
        compiler_params=pltpu.CompilerParams(
            dimension_semantics=("arbitrary", "arbitrary"), vmem_limit_bytes=VMEM_LIMIT),
        name="mixer_out",
    )(x, o, gc, sgb, mod.reshape(b, 1, N_MOD * d), gains, w_proj_attn, w_out, w1, w2)


def kernel(x, c, w_ada, b_ada, g_pre_mix, g_post_mix, g_pre_mlp, g_post_mlp, w_in, conv_w,
           w_proj_conv, w_proj_attn, w_out, w_mlp_in, w_mlp_out):
    depth = w_ada.shape[0]
    mod = _ada(c, w_ada, b_ada)
    w_mlp_out = w_mlp_out.astype(_bf16)
    for l in range(depth):
        q, k, v, gc, sgb = _mixer_in(l, x, mod[l], g_pre_mix[l], w_in, conv_w, w_proj_conv)
        o = _attention(q, k, v)
        gains = jnp.stack([g_post_mix[l], g_pre_mlp[l], g_post_mlp[l]])
        x = _mixer_out(l, x, o, gc, sgb, mod[l], gains, w_proj_attn, w_out, w_mlp_in, w_mlp_out)
    return x
```

```python
import functools
import math

import jax
import jax.numpy as jnp
from jax import lax
from jax.experimental import pallas as pl
from jax.experimental.pallas import tpu as pltpu

CONV_WIDTH = 512
CONV_TAPS = 3
ATTN_WIDTH = 512
HEAD_DIM = 64
N_MOD = 6
EPS = 1e-6

LANES = 128
SUBLANES = 8
HEADS_PER_TILE = LANES // HEAD_DIM
ATTN_LANE_TILES = 4
SEQ_TILE = 512
Q_TILE = 256
K_TILE = 256
ADA_COLS = 2048
FF_CHUNK = 1024
ROW_SUBTILES = 2
VMEM_LIMIT = 56 * 1024 * 1024
LOG2E = math.log2(math.e)
UNDERFLOW_LOG2 = 152.0
FUSED_BLOCKS = 3
HEAD_SKEW = 2
VALUE_SKEW = 4
TAIL_ROW_GROUPS = 2
MASKED_LOGIT = -1e30
Z_CLAMP = 120.0

_f32 = jnp.float32
_bf16 = jnp.bfloat16


def _dot(a, b):
    return jnp.dot(a, b, preferred_element_type=_f32)


def _rms(x, gain):
    return x * lax.rsqrt(jnp.mean(x * x, axis=-1, keepdims=True) + EPS) * gain


def _layer_spec(stacked, layer):
    rest = stacked.shape[1:]
    return pl.BlockSpec((None,) + rest, lambda *_: (layer,) + (0,) * len(rest),
                        pipeline_mode=pl.Buffered(1))


def _const_spec(shape):
    return pl.BlockSpec(shape, lambda *_: (0,) * len(shape), pipeline_mode=pl.Buffered(1))


def _ada_kernel(ct_ref, w_ref, b_ref, o_ref):
    w = w_ref[0]
    rows = [jnp.sum(w * ct_ref[:, i:i + 1], axis=0, keepdims=True) for i in range(ct_ref.shape[1])]
    o_ref[0] = jnp.concatenate(rows, axis=0) + b_ref[0, 0]


def _ada(c, w_ada, b_ada):
    depth, d, width = w_ada.shape
    b = c.shape[0]
    cols = ADA_COLS
    return pl.pallas_call(
        _ada_kernel,
        grid=(depth, width // cols),
        in_specs=[
            pl.BlockSpec((d, b), lambda l, j: (0, 0)),
            pl.BlockSpec((1, d, cols), lambda l, j: (l, 0, j)),
            pl.BlockSpec((1, 1, 1, cols), lambda l, j: (l, j, 0, 0)),
        ],
        out_specs=pl.BlockSpec((1, b, cols), lambda l, j: (l, 0, j)),
        out_shape=jax.ShapeDtypeStruct((depth, b, width), _f32),
        compiler_params=pltpu.CompilerParams(
            dimension_semantics=("arbitrary", "arbitrary"), vmem_limit_bytes=VMEM_LIMIT),
        name="ada",
    )(c.T, w_ada, b_ada.reshape(depth, width // cols, 1, cols))


def _mixer_in_kernel(x_ref, mod_ref, g_ref, w_in_ref, cw_ref, wpc_ref,
                     q_ref, k_ref, v_ref, gc_ref, sgb_ref, carry_ref):
    d = x_ref.shape[-1]
    tm = x_ref.shape[1]
    cw = CONV_WIDTH

    @pl.when(pl.program_id(1) == 0)
    def _():
        carry_ref[...] = jnp.zeros_like(carry_ref)

    shift = mod_ref[0, :, 0:d]
    scale = mod_ref[0, :, d:2 * d]
    h = _rms(x_ref[0], g_ref[...]) * (1.0 + scale) + shift

    bg = _dot(h, w_in_ref[:, 0:cw])
    cg = _dot(h, w_in_ref[:, cw:2 * cw])
    u = _dot(h, w_in_ref[:, 2 * cw:3 * cw])

    o = 3 * cw
    aw = ATTN_WIDTH
    q_ref[0] = (_dot(h, w_in_ref[:, o:o + aw]) * (LOG2E / math.sqrt(HEAD_DIM))).astype(_bf16)
    k_ref[0] = _dot(h, w_in_ref[:, o + aw:o + 2 * aw]).astype(_bf16)
    v_ref[0] = _dot(h, w_in_ref[:, o + 2 * aw:o + 3 * aw]).astype(_bf16)
    o += 3 * aw
    gb = _dot(h, w_in_ref[:, o + d:o + 2 * d])
    sgb_ref[0] = jax.nn.sigmoid(gb)
    ga = _dot(h, w_in_ref[:, o:o + d])

    vv = cg * u
    row = lax.broadcasted_iota(jnp.int32, vv.shape, 0)
    prev1 = carry_ref[1:2, :]
    prev2 = carry_ref[0:1, :]
    v1 = jnp.where(row == 0, prev1, pltpu.roll(vv, 1, axis=0))
    v2 = jnp.where(row == 0, prev2, jnp.where(row == 1, prev1, pltpu.roll(vv, 2, axis=0)))
    carry_ref[...] = vv[tm - 2:tm, :]
    conv = cw_ref[0:1, :] * v2 + cw_ref[1:2, :] * v1 + cw_ref[2:3, :] * vv
    y_conv = _dot(bg * conv, wpc_ref[...])
    gc_ref[0] = jax.nn.sigmoid(ga) * y_conv


def _mixer_in(layer, x, mod, g_pre, w_in, conv_w, w_proj_conv):
    b, s, d = x.shape
    tm = SEQ_TILE
    tile = lambda width: pl.BlockSpec((1, tm, width), lambda i, j: (i, j, 0))
    return pl.pallas_call(
        _mixer_in_kernel,
        grid=(b, s // tm),
        in_specs=[
            tile(d),
            pl.BlockSpec((1, 1, N_MOD * d), lambda i, j: (i, 0, 0)),
            _const_spec((1, d)),
            _layer_spec(w_in, layer),
            _layer_spec(conv_w, layer),
            _layer_spec(w_proj_conv, layer),
        ],
        out_specs=[tile(ATTN_WIDTH), tile(ATTN_WIDTH), tile(ATTN_WIDTH), tile(d), tile(d)],
        out_shape=[
            jax.ShapeDtypeStruct((b, s, ATTN_WIDTH), _bf16),
            jax.ShapeDtypeStruct((b, s, ATTN_WIDTH), _bf16),
            jax.ShapeDtypeStruct((b, s, ATTN_WIDTH), _bf16),
            jax.ShapeDtypeStruct((b, s, d), _f32),
            jax.ShapeDtypeStruct((b, s, d), _f32),
        ],
        scratch_shapes=[pltpu.VMEM((CONV_TAPS - 1, CONV_WIDTH), _f32)],
        compiler_params=pltpu.CompilerParams(
            dimension_semantics=("arbitrary", "arbitrary"), vmem_limit_bytes=VMEM_LIMIT),
        name="mixer_in",
    )(x, mod.reshape(b, 1, N_MOD * d), g_pre.reshape(1, d), w_in, conv_w, w_proj_conv)


def _attn_kernel(q_ref, k_ref, v_ref, o_ref, qh_ref, acc_ref, run_ref):
    tq = q_ref.shape[1]
    qi = pl.program_id(2)
    n_heads = qh_ref.shape[0]
    lanes_of = lambda h: slice((h // HEADS_PER_TILE) * LANES, (h // HEADS_PER_TILE + 1) * LANES)

    lane = lax.broadcasted_iota(jnp.int32, (tq, LANES), 1)
    own = [lane < HEAD_DIM, lane >= HEAD_DIM]
    for h in range(n_heads):
        q = q_ref[0, :, lanes_of(h)]
        qh_ref[h] = jnp.where(own[h % HEADS_PER_TILE], q, jnp.zeros_like(q))

    tri2 = (lax.broadcasted_iota(jnp.int32, (K_TILE, K_TILE), 0)
            > lax.broadcasted_iota(jnp.int32, (K_TILE, K_TILE), 1)).astype(_bf16)

    heads = range(n_heads)

    def rows_of(kb):
        return pl.ds(pl.multiple_of(kb * K_TILE, K_TILE), K_TILE)

    every_row = slice(0, tq)

    def logits(kb, h, rows=every_row):
        return lax.dot_general(qh_ref[h, rows, :], k_ref[0, rows_of(kb), lanes_of(h)],
                               (((1,), (1,)), ((), ())), preferred_element_type=_f32)

    def suffix_sums(z, diagonal):
        sp = jnp.maximum(jnp.log2(1.0 + jnp.exp2(jnp.minimum(z, Z_CLAMP))), z)
        log_beta = z - sp
        if diagonal:
            keep = (lax.broadcasted_iota(jnp.int32, z.shape, 1)
                    < lax.broadcasted_iota(jnp.int32, z.shape, 0))
            sp = jnp.where(keep, sp, 0.0)
            log_beta = jnp.where(keep, log_beta, MASKED_LOGIT)
        return log_beta, (_dot(sp.astype(_bf16), tri2), sp[:, 0:1])

    def accumulate(kb, h, log_beta, sums, first, rows=every_row):
        right_of, leftmost = sums
        block_sum = jnp.broadcast_to(right_of[:, 0:1] + leftmost, (log_beta.shape[0], LANES))
        values = v_ref[0, rows_of(kb), lanes_of(h)]
        if first:
            acc_ref[h, rows, :] = _dot(jnp.exp2(log_beta - right_of).astype(_bf16), values)
            run_ref[h, rows, :] = block_sum
        else:
            run = run_ref[h, rows, :]
            a = jnp.exp2(log_beta - right_of - jnp.concatenate([run] * (K_TILE // LANES), axis=1))
            acc_ref[h, rows, :] += _dot(a.astype(_bf16), values)
            run_ref[h, rows, :] = run + block_sum

    def run_min(rows):
        parts = [run_ref[h, rows, :] for h in heads]
        while len(parts) > 1:
            parts = [jnp.minimum(a, b) for a, b in zip(parts[0::2], parts[1::2])]
        m = parts[0]
        while m.shape[0] > SUBLANES:
            half = m.shape[0] // 2
            m = jnp.minimum(m[:half], m[half:])
        return jnp.min(m)

    def sweep(blocks, tile_start, value_skew, rows=every_row):
        work = [(i, h) for i in range(len(blocks)) for h in heads]
        z, staged = {}, {}
        for t in range(len(work) + value_skew):
            if 0 <= t - value_skew:
                i, h = work[t - value_skew]
                accumulate(blocks[i], h, *staged.pop((i, h)), first=(tile_start and i == 0), rows=rows)
            if t < len(work):
                i, h = work[t]
                z[i, h] = logits(blocks[i], h, rows)
            if 0 <= t - HEAD_SKEW < len(work):
                i, h = work[t - HEAD_SKEW]
                staged[i, h] = suffix_sums(z.pop((i, h)), diagonal=(tile_start and i == 0))

    @pl.when(qi >= FUSED_BLOCKS - 1)
    def _():
        sweep([qi - i for i in range(FUSED_BLOCKS)], tile_start=True, value_skew=VALUE_SKEW)

    @pl.when(qi < FUSED_BLOCKS - 1)
    def _():
        sweep([qi], tile_start=True, value_skew=VALUE_SKEW)

    row_groups = [slice(r, r + tq // TAIL_ROW_GROUPS) for r in range(0, tq, tq // TAIL_ROW_GROUPS)]

    def run_mins():
        return tuple(run_min(rows) for rows in row_groups)

    def more(carry):
        kb, mins = carry
        return jnp.logical_and(kb >= 0, functools.reduce(jnp.minimum, mins) < UNDERFLOW_LOG2)

    def body(carry):
        kb, mins = carry
        for rows, group_min in zip(row_groups, mins):
            @pl.when(group_min < UNDERFLOW_LOG2)
            def _():
                sweep([kb], tile_start=False, value_skew=n_heads, rows=rows)
        return kb - 1, run_mins()

    first_left = jnp.where(qi >= FUSED_BLOCKS - 1, qi - FUSED_BLOCKS, qi - 1)
    lax.while_loop(more, body, (first_left, run_mins()))

    for h in range(0, n_heads, HEADS_PER_TILE):
        o_ref[0, :, lanes_of(h)] = jnp.where(own[0], acc_ref[h], acc_ref[h + 1]).astype(o_ref.dtype)


def _attention(q, k, v):
    b, s, w = q.shape
    width = ATTN_LANE_TILES * LANES
    n_heads = ATTN_LANE_TILES * HEADS_PER_TILE
    assert Q_TILE == K_TILE and s % Q_TILE == 0 and w % width == 0
    seq = pl.BlockSpec((1, s, width), lambda i, h, j: (i, 0, h))
    tile = pl.BlockSpec((1, Q_TILE, width), lambda i, h, j: (i, j, h))
    return pl.pallas_call(
        _attn_kernel,
        grid=(b, w // width, s // Q_TILE),
        in_specs=[tile, seq, seq],
        out_specs=tile,
        out_shape=jax.ShapeDtypeStruct((b, s, w), _bf16),
        scratch_shapes=[
            pltpu.VMEM((n_heads, Q_TILE, LANES), _bf16),
            pltpu.VMEM((n_heads, Q_TILE, LANES), _f32),
            pltpu.VMEM((n_heads, Q_TILE, LANES), _f32),
        ],
        compiler_params=pltpu.CompilerParams(
            dimension_semantics=("arbitrary", "arbitrary", "arbitrary"),
            vmem_limit_bytes=VMEM_LIMIT),
        name="attention",
    )(q, k, v)


def _mixer_out_kernel(x_ref, o_ref, gc_ref, sgb_ref, mod_ref, gains_ref,
                      wpa_ref, wout_ref, w1_ref, w2_ref, out_ref):
    d = x_ref.shape[-1]
    mod = lambda i: mod_ref[0, :, i * d:(i + 1) * d]
    gate1, shift2, scale2, gate2 = mod(2), mod(3), mod(4), mod(5)
    g_post_mix = gains_ref[0:1, :]
    g_pre_mlp = gains_ref[1:2, :]
    g_post_mlp = gains_ref[2:3, :]

    tm = x_ref.shape[1]
    subs = [pl.ds(r, tm // ROW_SUBTILES) for r in range(0, tm, tm // ROW_SUBTILES)]

    mix = []
    for rows in subs:
        y_attn = _dot(o_ref[0, rows, :], wpa_ref[...])
        merged = gc_ref[0, rows, :] + sgb_ref[0, rows, :] * y_attn
        mix.append(_dot(merged, wout_ref[...]))
    x1, h2 = [], []
    for i, rows in enumerate(subs):
        x1.append(x_ref[0, rows, :] + gate1 * _rms(mix[i], g_post_mix))
        h2.append(_rms(x1[i], g_pre_mlp) * (1.0 + scale2) + shift2)
    for i, rows in enumerate(subs):
        ff = jnp.zeros_like(x1[i])
        for j in range(0, w1_ref.shape[1], FF_CHUNK):
            hid = jnp.maximum(_dot(h2[i], w1_ref[:, j:j + FF_CHUNK]), 0.0)
            ff = ff + _dot((hid * hid).astype(_bf16), w2_ref[j:j + FF_CHUNK, :])
        out_ref[0, rows, :] = x1[i] + gate2 * _rms(ff, g_post_mlp)


def _mixer_out(layer, x, o, gc, sgb, mod, gains, w_proj_attn, w_out, w1, w2):
    b, s, d = x.shape
    tm = SEQ_TILE
    tile = lambda width: pl.BlockSpec((1, tm, width), lambda i, j: (i, j, 0))
    return pl.pallas_call(
        _mixer_out_kernel,
        grid=(b, s // tm),
        in_specs=[
            tile(d), tile(ATTN_WIDTH), tile(d), tile(d),
            pl.BlockSpec((1, 1, N_MOD * d), lambda i, j: (i, 0, 0)),
            _const_spec(gains.shape),
            _layer_spec(w_proj_attn, layer),
            _layer_spec(w_out, layer),
            _layer_spec(w1, layer),
            _layer_spec(w2, layer),
        ],
        out_specs=tile(d),
        out_shape=jax.ShapeDtypeStruct((b, s, d), _f32),
        compiler_params=pltpu.CompilerParams(
            dimension_semantics=("arbitrary", "arbitrary"), vmem_limit_bytes=VMEM_LIMIT),
        name="mixer_out",
    )(x, o, gc, sgb, mod.reshape(b, 1, N_MOD * d), gains, w_proj_attn, w_out, w1, w2)


def kernel(x, c, w_ada, b_ada, g_pre_mix, g_post_mix, g_pre_mlp, g_post_mlp, w_in, conv_w,
           w_proj_conv, w_proj_attn, w_out, w_mlp_in, w_mlp_out):
    depth = w_ada.shape[0]
    mod = _ada(c, w_ada, b_ada)
    w_mlp_out = w_mlp_out.astype(_bf16)
    for l in range(depth):
        q, k, v, gc, sgb = _mixer_in(l, x, mod[l], g_pre_mix[l], w_in, conv_w, w_proj_conv)
        o = _attention(q, k, v)
        gains = jnp.stack([g_post_mix[l], g_pre_mlp[l], g_post_mlp[l]])
        x = _mixer_out(l, x, o, gc, sgb, mod[l], gains, w_proj_attn, w_out, w_mlp_in, w_mlp_out)
    return x
```

```python
import functools
import math

import jax
import jax.numpy as jnp
from jax import lax
from jax.experimental import pallas as pl
from jax.experimental.pallas import tpu as pltpu

CONV_WIDTH = 512
CONV_TAPS = 3
ATTN_WIDTH = 512
HEAD_DIM = 64
N_MOD = 6
EPS = 1e-6

LANES = 128
SUBLANES = 8
HEADS_PER_TILE = LANES // HEAD_DIM
ATTN_LANE_TILES = 4
SEQ_TILE = 512
Q_TILE = 256
K_TILE = 256
ADA_COLS = 2048
FF_CHUNK = 1024
ROW_SUBTILES = 2
VMEM_LIMIT = 56 * 1024 * 1024
LOG2E = math.log2(math.e)
UNDERFLOW_LOG2 = 152.0
FUSED_BLOCKS = 3
HEAD_SKEW = 2
VALUE_SKEW = 4
TAIL_ROW_GROUPS = 2
MASKED_LOGIT = -1e30
Z_CLAMP = 120.0

_f32 = jnp.float32
_bf16 = jnp.bfloat16


def _dot(a, b):
    return jnp.dot(a, b, preferred_element_type=_f32)


def _rms(x, gain):
    return x * lax.rsqrt(jnp.mean(x * x, axis=-1, keepdims=True) + EPS) * gain


def _layer_spec(stacked, layer):
    rest = stacked.shape[1:]
    return pl.BlockSpec((None,) + rest, lambda *_: (layer,) + (0,) * len(rest),
                        pipeline_mode=pl.Buffered(1))


def _const_spec(shape):
    return pl.BlockSpec(shape, lambda *_: (0,) * len(shape), pipeline_mode=pl.Buffered(1))


def _ada_kernel(ct_ref, w_ref, b_ref, o_ref):
    w = w_ref[0]
    rows = [jnp.sum(w * ct_ref[:, i:i + 1], axis=0, keepdims=True) for i in range(ct_ref.shape[1])]
    o_ref[0] = jnp.concatenate(rows, axis=0) + b_ref[0, 0]


def _ada(c, w_ada, b_ada):
    depth, d, width = w_ada.shape
    b = c.shape[0]
    cols = ADA_COLS
    assert width % cols == 0
    return pl.pallas_call(
        _ada_kernel,
        grid=(depth, width // cols),
        in_specs=[
            pl.BlockSpec((d, b), lambda l, j: (0, 0)),
            pl.BlockSpec((1, d, cols), lambda l, j: (l, 0, j)),
            pl.BlockSpec((1, 1, 1, cols), lambda l, j: (l, j, 0, 0)),
        ],
        out_specs=pl.BlockSpec((1, b, cols), lambda l, j: (l, 0, j)),
        out_shape=jax.ShapeDtypeStruct((depth, b, width), _f32),
        compiler_params=pltpu.CompilerParams(
            dimension_semantics=("arbitrary", "arbitrary"), vmem_limit_bytes=VMEM_LIMIT),
        name="ada",
    )(c.T, w_ada, b_ada.reshape(depth, width // cols, 1, cols))


def _mixer_in_kernel(x_ref, mod_ref, g_ref, w_in_ref, cw_ref, wpc_ref,
                     q_ref, k_ref, v_ref, gc_ref, sgb_ref, carry_ref):
    d = x_ref.shape[-1]
    tm = x_ref.shape[1]
    cw = CONV_WIDTH

    @pl.when(pl.program_id(1) == 0)
    def _():
        carry_ref[...] = jnp.zeros_like(carry_ref)

    shift = mod_ref[0, :, 0:d]
    scale = mod_ref[0, :, d:2 * d]
    h = _rms(x_ref[0], g_ref[...]) * (1.0 + scale) + shift

    bg = _dot(h, w_in_ref[:, 0:cw])
    cg = _dot(h, w_in_ref[:, cw:2 * cw])
    u = _dot(h, w_in_ref[:, 2 * cw:3 * cw])

    o = 3 * cw
    aw = ATTN_WIDTH
    q = _dot(h, w_in_ref[:, o:o + aw]) * (LOG2E / math.sqrt(HEAD_DIM))
    head_in_tile = (lax.broadcasted_iota(jnp.int32, q.shape, 1) // HEAD_DIM) % HEADS_PER_TILE
    for p in range(HEADS_PER_TILE):
        q_ref[0, p] = jnp.where(head_in_tile == p, q, 0.0).astype(_bf16)
    k_ref[0] = _dot(h, w_in_ref[:, o + aw:o + 2 * aw]).astype(_bf16)
    v_ref[0] = _dot(h, w_in_ref[:, o + 2 * aw:o + 3 * aw]).astype(_bf16)
    o += 3 * aw
    gb = _dot(h, w_in_ref[:, o + d:o + 2 * d])
    sgb_ref[0] = jax.nn.sigmoid(gb)
    ga = _dot(h, w_in_ref[:, o:o + d])

    vv = cg * u
    row = lax.broadcasted_iota(jnp.int32, vv.shape, 0)
    prev1 = carry_ref[1:2, :]
    prev2 = carry_ref[0:1, :]
    v1 = jnp.where(row == 0, prev1, pltpu.roll(vv, 1, axis=0))
    v2 = jnp.where(row == 0, prev2, jnp.where(row == 1, prev1, pltpu.roll(vv, 2, axis=0)))
    carry_ref[...] = vv[tm - 2:tm, :]
    conv = cw_ref[0:1, :] * v2 + cw_ref[1:2, :] * v1 + cw_ref[2:3, :] * vv
    y_conv = _dot(bg * conv, wpc_ref[...])
    gc_ref[0] = jax.nn.sigmoid(ga) * y_conv


def _mixer_in(layer, x, mod, g_pre, w_in, conv_w, w_proj_conv):
    b, s, d = x.shape
    tm = SEQ_TILE
    assert s % tm == 0 and w_in.shape[-1] == 3 * CONV_WIDTH + 3 * ATTN_WIDTH + 2 * d
    assert conv_w.shape[1:] == (CONV_TAPS, CONV_WIDTH)
    tile = lambda width: pl.BlockSpec((1, tm, width), lambda i, j: (i, j, 0))
    return pl.pallas_call(
        _mixer_in_kernel,
        grid=(b, s // tm),
        in_specs=[
            tile(d),
            pl.BlockSpec((1, 1, N_MOD * d), lambda i, j: (i, 0, 0)),
            _const_spec((1, d)),
            _layer_spec(w_in, layer),
            _layer_spec(conv_w, layer),
            _layer_spec(w_proj_conv, layer),
        ],
        out_specs=[
            pl.BlockSpec((1, HEADS_PER_TILE, tm, ATTN_WIDTH), lambda i, j: (i, 0, j, 0)),
            tile(ATTN_WIDTH), tile(ATTN_WIDTH), tile(d), tile(d),
        ],
        out_shape=[
            jax.ShapeDtypeStruct((b, HEADS_PER_TILE, s, ATTN_WIDTH), _bf16),
            jax.ShapeDtypeStruct((b, s, ATTN_WIDTH), _bf16),
            jax.ShapeDtypeStruct((b, s, ATTN_WIDTH), _bf16),
            jax.ShapeDtypeStruct((b, s, d), _f32),
            jax.ShapeDtypeStruct((b, s, d), _f32),
        ],
        scratch_shapes=[pltpu.VMEM((CONV_TAPS - 1, CONV_WIDTH), _f32)],
        compiler_params=pltpu.CompilerParams(
            dimension_semantics=("arbitrary", "arbitrary"), vmem_limit_bytes=VMEM_LIMIT),
        name="mixer_in",
    )(x, mod.reshape(b, 1, N_MOD * d), g_pre.reshape(1, d), w_in, conv_w, w_proj_conv)


def _attn_kernel(q_ref, k_ref, v_ref, o_ref, acc_ref, run_ref, kbuf_ref, vbuf_ref):
    tq = q_ref.shape[2]
    qi = pl.program_id(2)
    n_heads = acc_ref.shape[0]
    lanes_of = lambda h: slice((h // HEADS_PER_TILE) * LANES, (h // HEADS_PER_TILE + 1) * LANES)

    tri2 = (lax.broadcasted_iota(jnp.int32, (K_TILE, K_TILE), 0)
            > lax.broadcasted_iota(jnp.int32, (K_TILE, K_TILE), 1)).astype(_bf16)

    heads = range(n_heads)

    def rows_of(kb):
        return pl.ds(pl.multiple_of(kb * K_TILE, K_TILE), K_TILE)

    every_row = slice(0, tq)

    kbuf_ref[rows_of(qi), :] = k_ref[0]
    vbuf_ref[rows_of(qi), :] = v_ref[0]

    def keys_values(kb, h, own_tile):
        if own_tile:
            return k_ref[0, :, lanes_of(h)], v_ref[0, :, lanes_of(h)]
        return kbuf_ref[rows_of(kb), lanes_of(h)], vbuf_ref[rows_of(kb), lanes_of(h)]

    def logits(kb, h, own_tile, rows=every_row):
        return lax.dot_general(q_ref[0, h % HEADS_PER_TILE, rows, lanes_of(h)],
                               keys_values(kb, h, own_tile)[0],
                               (((1,), (1,)), ((), ())), preferred_element_type=_f32)

    def suffix_sums(z, diagonal):
        sp = jnp.maximum(jnp.log2(1.0 + jnp.exp2(jnp.minimum(z, Z_CLAMP))), z)
        log_beta = z - sp
        if diagonal:
            keep = (lax.broadcasted_iota(jnp.int32, z.shape, 1)
                    < lax.broadcasted_iota(jnp.int32, z.shape, 0))
            sp = jnp.where(keep, sp, 0.0)
            log_beta = jnp.where(keep, log_beta, MASKED_LOGIT)
        return log_beta, (_dot(sp.astype(_bf16), tri2), sp[:, 0:1])

    def accumulate(kb, h, log_beta, sums, first, rows=every_row):
        right_of, leftmost = sums
        block_sum = jnp.broadcast_to(right_of[:, 0:1] + leftmost, (log_beta.shape[0], LANES))
        values = keys_values(kb, h, own_tile=first)[1]
        if first:
            acc_ref[h, rows, :] = _dot(jnp.exp2(log_beta - right_of).astype(_bf16), values)
            run_ref[h, rows, :] = block_sum
        else:
            run = run_ref[h, rows, :]
            a = jnp.exp2(log_beta - right_of - jnp.concatenate([run] * (K_TILE // LANES), axis=1))
            acc_ref[h, rows, :] += _dot(a.astype(_bf16), values)
            run_ref[h, rows, :] = run + block_sum

    def run_min(rows):
        parts = [run_ref[h, rows, :] for h in heads]
        while len(parts) > 1:
            parts = [jnp.minimum(a, b) for a, b in zip(parts[0::2], parts[1::2])]
        m = parts[0]
        while m.shape[0] > SUBLANES:
            half = m.shape[0] // 2
            m = jnp.minimum(m[:half], m[half:])
        return jnp.min(m)

    def sweep(blocks, tile_start, value_skew, rows=every_row):
        work = [(i, h) for i in range(len(blocks)) for h in heads]
        z, staged = {}, {}
        for t in range(len(work) + value_skew):
            if 0 <= t - value_skew:
                i, h = work[t - value_skew]
                accumulate(blocks[i], h, *staged.pop((i, h)), first=(tile_start and i == 0), rows=rows)
            if t < len(work):
                i, h = work[t]
                z[i, h] = logits(blocks[i], h, own_tile=(tile_start and i == 0), rows=rows)
            if 0 <= t - HEAD_SKEW < len(work):
                i, h = work[t - HEAD_SKEW]
                staged[i, h] = suffix_sums(z.pop((i, h)), diagonal=(tile_start and i == 0))

    @pl.when(qi >= FUSED_BLOCKS - 1)
    def _():
        sweep([qi - i for i in range(FUSED_BLOCKS)], tile_start=True, value_skew=VALUE_SKEW)

    @pl.when(qi < FUSED_BLOCKS - 1)
    def _():
        sweep([qi], tile_start=True, value_skew=VALUE_SKEW)

    row_groups = [slice(r, r + tq // TAIL_ROW_GROUPS) for r in range(0, tq, tq // TAIL_ROW_GROUPS)]

    def run_mins():
        return tuple(run_min(rows) for rows in row_groups)

    def more(carry):
        kb, mins = carry
        return jnp.logical_and(kb >= 0, functools.reduce(jnp.minimum, mins) < UNDERFLOW_LOG2)

    def body(carry):
        kb, mins = carry
        for rows, group_min in zip(row_groups, mins):
            @pl.when(group_min < UNDERFLOW_LOG2)
            def _():
                sweep([kb], tile_start=False, value_skew=n_heads, rows=rows)
        return kb - 1, run_mins()

    first_left = jnp.where(qi >= FUSED_BLOCKS - 1, qi - FUSED_BLOCKS, qi - 1)
    lax.while_loop(more, body, (first_left, run_mins()))

    first_head = lax.broadcasted_iota(jnp.int32, (tq, LANES), 1) < HEAD_DIM
    for h in range(0, n_heads, HEADS_PER_TILE):
        o_ref[0, :, lanes_of(h)] = jnp.where(first_head, acc_ref[h], acc_ref[h + 1]).astype(o_ref.dtype)


def _attention(q, k, v):
    b, s, w = k.shape
    width = ATTN_LANE_TILES * LANES
    n_heads = ATTN_LANE_TILES * HEADS_PER_TILE
    assert HEADS_PER_TILE == 2 and Q_TILE == K_TILE and s % Q_TILE == 0 and w % width == 0
    tile = pl.BlockSpec((1, Q_TILE, width), lambda i, h, j: (i, j, h))
    return pl.pallas_call(
        _attn_kernel,
        grid=(b, w // width, s // Q_TILE),
        in_specs=[pl.BlockSpec((1, HEADS_PER_TILE, Q_TILE, width), lambda i, h, j: (i, 0, j, h)),
                  tile, tile],
        out_specs=tile,
        out_shape=jax.ShapeDtypeStruct((b, s, w), _bf16),
        scratch_shapes=[
            pltpu.VMEM((n_heads, Q_TILE, LANES), _f32),
            pltpu.VMEM((n_heads, Q_TILE, LANES), _f32),
            pltpu.VMEM((s, width), _bf16),
            pltpu.VMEM((s, width), _bf16),
        ],
        compiler_params=pltpu.CompilerParams(
            dimension_semantics=("arbitrary", "arbitrary", "arbitrary"),
            vmem_limit_bytes=VMEM_LIMIT),
        name="attention",
    )(q, k, v)


def _mixer_out_kernel(x_ref, o_ref, gc_ref, sgb_ref, mod_ref, gains_ref,
                      wpa_ref, wout_ref, w1_ref, w2_ref, out_ref):
    d = x_ref.shape[-1]
    mod = lambda i: mod_ref[0, :, i * d:(i + 1) * d]
    gate1, shift2, scale2, gate2 = mod(2), mod(3), mod(4), mod(5)
    g_post_mix = gains_ref[0:1, :]
    g_pre_mlp = gains_ref[1:2, :]
    g_post_mlp = gains_ref[2:3, :]

    tm = x_ref.shape[1]
    subs = [pl.ds(r, tm // ROW_SUBTILES) for r in range(0, tm, tm // ROW_SUBTILES)]

    mix = []
    for rows in subs:
        y_attn = _dot(o_ref[0, rows, :], wpa_ref[...])
        merged = gc_ref[0, rows, :] + sgb_ref[0, rows, :] * y_attn
        mix.append(_dot(merged, wout_ref[...]))
    x1, h2 = [], []
    for i, rows in enumerate(subs):
        x1.append(x_ref[0, rows, :] + gate1 * _rms(mix[i], g_post_mix))
        h2.append(_rms(x1[i], g_pre_mlp) * (1.0 + scale2) + shift2)
    for i, rows in enumerate(subs):
        ff = jnp.zeros_like(x1[i])
        for j in range(0, w1_ref.shape[1], FF_CHUNK):
            hid = jnp.maximum(_dot(h2[i], w1_ref[:, j:j + FF_CHUNK]), 0.0)
            ff = ff + _dot((hid * hid).astype(_bf16), w2_ref[j:j + FF_CHUNK, :])
        out_ref[0, rows, :] = x1[i] + gate2 * _rms(ff, g_post_mlp)


def _mixer_out(layer, x, o, gc, sgb, mod, gains, w_proj_attn, w_out, w1, w2):
    b, s, d = x.shape
    tm = SEQ_TILE
    assert s % tm == 0 and tm % (ROW_SUBTILES * SUBLANES) == 0 and w1.shape[-1] % FF_CHUNK == 0
    tile = lambda width: pl.BlockSpec((1, tm, width), lambda i, j: (i, j, 0))
    return pl.pallas_call(
        _mixer_out_kernel,
        grid=(b, s // tm),
        in_specs=[
            tile(d), tile(ATTN_WIDTH), tile(d), tile(d),
            pl.BlockSpec((1, 1, N_MOD * d), lambda i, j: (i, 0, 0)),
            _const_spec(gains.shape),
            _layer_spec(w_proj_attn, layer),
            _layer_spec(w_out, layer),
            _layer_spec(w1, layer),
            _layer_spec(w2, layer),
        ],
        out_specs=tile(d),
        out_shape=jax.ShapeDtypeStruct((b, s, d), _f32),
        compiler_params=pltpu.CompilerParams(
            dimension_semantics=("arbitrary", "arbitrary"), vmem_limit_bytes=VMEM_LIMIT),
        name="mixer_out",
    )(x, o, gc, sgb, mod.reshape(b, 1, N_MOD * d), gains, w_proj_attn, w_out, w1, w2)


def kernel(x, c, w_ada, b_ada, g_pre_mix, g_post_mix, g_pre_mlp, g_post_mlp, w_in, conv_w,
           w_proj_conv, w_proj_attn, w_out, w_mlp_in, w_mlp_out):
    depth = w_ada.shape[0]
    mod = _ada(c, w_ada, b_ada)
    w_mlp_out = w_mlp_out.astype(_bf16)
    for l in range(depth):
        q, k, v, gc, sgb = _mixer_in(l, x, mod[l], g_pre_mix[l], w_in, conv_w, w_proj_conv)
        o = _attention(q, k, v)
        gains = jnp.stack([g_post_mix[l], g_pre_mlp[l], g_post_mlp[l]])
        x = _mixer_out(l, x, o, gc, sgb, mod[l], gains, w_proj_attn, w_out, w_mlp_in, w_mlp_out)
    return x
```

```python
import functools
import math

import jax
import jax.numpy as jnp
from jax import lax
from jax.experimental import pallas as pl
from jax.experimental.pallas import tpu as pltpu

CONV_WIDTH = 512
CONV_TAPS = 3
ATTN_WIDTH = 512
HEAD_DIM = 64
N_MOD = 6
EPS = 1e-6

LANES = 128
SUBLANES = 8
HEADS_PER_TILE = LANES // HEAD_DIM
ATTN_LANE_TILES = 4
SEQ_TILE = 512
Q_TILE = 256
K_TILE = 256
ADA_COLS = 2048
FF_CHUNK = 1024
ROW_SUBTILES = 2
VMEM_LIMIT = 56 * 1024 * 1024
LOG2E = math.log2(math.e)
UNDERFLOW_LOG2 = 152.0
FUSED_BLOCKS = 3
HEAD_SKEW = 2
VALUE_SKEW = 4
TAIL_ROW_GROUPS = 2
MASKED_LOGIT = -1e30
Z_CLAMP = 120.0

_f32 = jnp.float32
_bf16 = jnp.bfloat16


def _dot(a, b):
    return jnp.dot(a, b, preferred_element_type=_f32)


def _rms(x, gain):
    return x * lax.rsqrt(jnp.mean(x * x, axis=-1, keepdims=True) + EPS) * gain


def _layer_spec(stacked, layer):
    rest = stacked.shape[1:]
    return pl.BlockSpec((None,) + rest, lambda *_: (layer,) + (0,) * len(rest),
                        pipeline_mode=pl.Buffered(1))


def _const_spec(shape):
    return pl.BlockSpec(shape, lambda *_: (0,) * len(shape), pipeline_mode=pl.Buffered(1))


def _ada_kernel(ct_ref, w_ref, b_ref, o_ref):
    w = w_ref[0]
    rows = [jnp.sum(w * ct_ref[:, i:i + 1], axis=0, keepdims=True) for i in range(ct_ref.shape[1])]
    o_ref[0] = jnp.concatenate(rows, axis=0) + b_ref[0, 0]


def _ada(c, w_ada, b_ada):
    depth, d, width = w_ada.shape
    b = c.shape[0]
    cols = ADA_COLS
    assert width % cols == 0
    return pl.pallas_call(
        _ada_kernel,
        grid=(depth, width // cols),
        in_specs=[
            pl.BlockSpec((d, b), lambda l, j: (0, 0)),
            pl.BlockSpec((1, d, cols), lambda l, j: (l, 0, j)),
            pl.BlockSpec((1, 1, 1, cols), lambda l, j: (l, j, 0, 0)),
        ],
        out_specs=pl.BlockSpec((1, b, cols), lambda l, j: (l, 0, j)),
        out_shape=jax.ShapeDtypeStruct((depth, b, width), _f32),
        compiler_params=pltpu.CompilerParams(
            dimension_semantics=("arbitrary", "arbitrary"), vmem_limit_bytes=VMEM_LIMIT),
        name="ada",
    )(c.T, w_ada, b_ada.reshape(depth, width // cols, 1, cols))


def _mixer_in_kernel(x_ref, mod_ref, g_ref, w_in_ref, cw_ref, wpc_ref, w1_ref, w2_ref,
                     q_ref, k_ref, v_ref, gc_ref, sgb_ref, w1_bf16_ref, w2_bf16_ref, carry_ref):
    d = x_ref.shape[-1]
    tm = x_ref.shape[1]
    cw = CONV_WIDTH

    w1_bf16_ref[...] = w1_ref[...].astype(_bf16)
    w2_bf16_ref[...] = w2_ref[...].astype(_bf16)

    @pl.when(pl.program_id(1) == 0)
    def _():
        carry_ref[...] = jnp.zeros_like(carry_ref)

    shift = mod_ref[0, :, 0:d]
    scale = mod_ref[0, :, d:2 * d]
    h = _rms(x_ref[0], g_ref[...]) * (1.0 + scale) + shift

    bg = _dot(h, w_in_ref[:, 0:cw])
    cg = _dot(h, w_in_ref[:, cw:2 * cw])
    u = _dot(h, w_in_ref[:, 2 * cw:3 * cw])

    o = 3 * cw
    aw = ATTN_WIDTH
    q = _dot(h, w_in_ref[:, o:o + aw]) * (LOG2E / math.sqrt(HEAD_DIM))
    head_in_tile = (lax.broadcasted_iota(jnp.int32, q.shape, 1) // HEAD_DIM) % HEADS_PER_TILE
    for p in range(HEADS_PER_TILE):
        q_ref[0, p] = jnp.where(head_in_tile == p, q, 0.0).astype(_bf16)
    k_ref[0] = _dot(h, w_in_ref[:, o + aw:o + 2 * aw]).astype(_bf16)
    v_ref[0] = _dot(h, w_in_ref[:, o + 2 * aw:o + 3 * aw]).astype(_bf16)
    o += 3 * aw
    gb = _dot(h, w_in_ref[:, o + d:o + 2 * d])
    sgb_ref[0] = jax.nn.sigmoid(gb)
    ga = _dot(h, w_in_ref[:, o:o + d])

    vv = cg * u
    row = lax.broadcasted_iota(jnp.int32, vv.shape, 0)
    prev1 = carry_ref[1:2, :]
    prev2 = carry_ref[0:1, :]
    v1 = jnp.where(row == 0, prev1, pltpu.roll(vv, 1, axis=0))
    v2 = jnp.where(row == 0, prev2, jnp.where(row == 1, prev1, pltpu.roll(vv, 2, axis=0)))
    carry_ref[...] = vv[tm - 2:tm, :]
    conv = cw_ref[0:1, :] * v2 + cw_ref[1:2, :] * v1 + cw_ref[2:3, :] * vv
    y_conv = _dot(bg * conv, wpc_ref[...])
    gc_ref[0] = jax.nn.sigmoid(ga) * y_conv


def _mixer_in(layer, x, mod, g_pre, w_in, conv_w, w_proj_conv, w1, w2):
    b, s, d = x.shape
    tm = SEQ_TILE
    assert s % tm == 0 and w_in.shape[-1] == 3 * CONV_WIDTH + 3 * ATTN_WIDTH + 2 * d
    assert conv_w.shape[1:] == (CONV_TAPS, CONV_WIDTH)
    steps = s // tm
    n_slices = b * steps
    bf16_rows = 2 * SUBLANES
    assert w1.shape[1] % (n_slices * bf16_rows) == 0 and w2.shape[1] % (n_slices * bf16_rows) == 0
    tile = lambda width: pl.BlockSpec((1, tm, width), lambda i, j: (i, j, 0))
    slice_in = lambda w: pl.BlockSpec((None, w.shape[1] // n_slices, w.shape[2]),
                                      lambda i, j: (layer, i * steps + j, 0))
    slice_out = lambda w: pl.BlockSpec((w.shape[1] // n_slices, w.shape[2]), lambda i, j: (i * steps + j, 0))
    return pl.pallas_call(
        _mixer_in_kernel,
        grid=(b, steps),
        in_specs=[
            tile(d),
            pl.BlockSpec((1, 1, N_MOD * d), lambda i, j: (i, 0, 0)),
            _const_spec((1, d)),
            _layer_spec(w_in, layer),
            _layer_spec(conv_w, layer),
            _layer_spec(w_proj_conv, layer),
            slice_in(w1), slice_in(w2),
        ],
        out_specs=[
            pl.BlockSpec((1, HEADS_PER_TILE, tm, ATTN_WIDTH), lambda i, j: (i, 0, j, 0)),
            tile(ATTN_WIDTH), tile(ATTN_WIDTH), tile(d), tile(d),
            slice_out(w1), slice_out(w2),
        ],
        out_shape=[
            jax.ShapeDtypeStruct((b, HEADS_PER_TILE, s, ATTN_WIDTH), _bf16),
            jax.ShapeDtypeStruct((b, s, ATTN_WIDTH), _bf16),
            jax.ShapeDtypeStruct((b, s, ATTN_WIDTH), _bf16),
            jax.ShapeDtypeStruct((b, s, d), _f32),
            jax.ShapeDtypeStruct((b, s, d), _f32),
            jax.ShapeDtypeStruct(w1.shape[1:], _bf16),
            jax.ShapeDtypeStruct(w2.shape[1:], _bf16),
        ],
        scratch_shapes=[pltpu.VMEM((CONV_TAPS - 1, CONV_WIDTH), _f32)],
        compiler_params=pltpu.CompilerParams(
            dimension_semantics=("arbitrary", "arbitrary"), vmem_limit_bytes=VMEM_LIMIT),
        name="mixer_in",
    )(x, mod.reshape(b, 1, N_MOD * d), g_pre.reshape(1, d), w_in, conv_w, w_proj_conv, w1, w2)


def _attn_kernel(q_ref, k_ref, v_ref, o_ref, acc_ref, run_ref, kbuf_ref, vbuf_ref):
    tq = q_ref.shape[2]
    qi = pl.program_id(2)
    n_heads = acc_ref.shape[0]
    lanes_of = lambda h: slice((h // HEADS_PER_TILE) * LANES, (h // HEADS_PER_TILE + 1) * LANES)

    tri2 = (lax.broadcasted_iota(jnp.int32, (K_TILE, K_TILE), 0)
            > lax.broadcasted_iota(jnp.int32, (K_TILE, K_TILE), 1)).astype(_bf16)

    heads = range(n_heads)

    def rows_of(kb):
        return pl.ds(pl.multiple_of(kb * K_TILE, K_TILE), K_TILE)

    every_row = slice(0, tq)

    kbuf_ref[rows_of(qi), :] = k_ref[0]
    vbuf_ref[rows_of(qi), :] = v_ref[0]

    def keys_values(kb, h, own_tile):
        if own_tile:
            return k_ref[0, :, lanes_of(h)], v_ref[0, :, lanes_of(h)]
        return kbuf_ref[rows_of(kb), lanes_of(h)], vbuf_ref[rows_of(kb), lanes_of(h)]

    def logits(kb, h, own_tile, rows=every_row):
        return lax.dot_general(q_ref[0, h % HEADS_PER_TILE, rows, lanes_of(h)],
                               keys_values(kb, h, own_tile)[0],
                               (((1,), (1,)), ((), ())), preferred_element_type=_f32)

    def suffix_sums(z, diagonal):
        sp = jnp.maximum(jnp.log2(1.0 + jnp.exp2(jnp.minimum(z, Z_CLAMP))), z)
        log_beta = z - sp
        if diagonal:
            keep = (lax.broadcasted_iota(jnp.int32, z.shape, 1)
                    < lax.broadcasted_iota(jnp.int32, z.shape, 0))
            sp = jnp.where(keep, sp, 0.0)
            log_beta = jnp.where(keep, log_beta, MASKED_LOGIT)
        return log_beta, (_dot(sp.astype(_bf16), tri2), sp[:, 0:1])

    def accumulate(kb, h, log_beta, sums, first, rows=every_row):
        right_of, leftmost = sums
        block_sum = jnp.broadcast_to(right_of[:, 0:1] + leftmost, (log_beta.shape[0], LANES))
        values = keys_values(kb, h, own_tile=first)[1]
        if first:
            acc_ref[h, rows, :] = _dot(jnp.exp2(log_beta - right_of).astype(_bf16), values)
            run_ref[h, rows, :] = block_sum
        else:
            run = run_ref[h, rows, :]
            a = jnp.exp2(log_beta - right_of - jnp.concatenate([run] * (K_TILE // LANES), axis=1))
            acc_ref[h, rows, :] += _dot(a.astype(_bf16), values)
            run_ref[h, rows, :] = run + block_sum

    def run_min(rows):
        parts = [run_ref[h, rows, :] for h in heads]
        while len(parts) > 1:
            parts = [jnp.minimum(a, b) for a, b in zip(parts[0::2], parts[1::2])]
        m = parts[0]
        while m.shape[0] > SUBLANES:
            half = m.shape[0] // 2
            m = jnp.minimum(m[:half], m[half:])
        return jnp.min(m)

    def sweep(blocks, tile_start, value_skew, rows=every_row):
        work = [(i, h) for i in range(len(blocks)) for h in heads]
        z, staged = {}, {}
        for t in range(len(work) + value_skew):
            if 0 <= t - value_skew:
                i, h = work[t - value_skew]
                accumulate(blocks[i], h, *staged.pop((i, h)), first=(tile_start and i == 0), rows=rows)
            if t < len(work):
                i, h = work[t]
                z[i, h] = logits(blocks[i], h, own_tile=(tile_start and i == 0), rows=rows)
            if 0 <= t - HEAD_SKEW < len(work):
                i, h = work[t - HEAD_SKEW]
                staged[i, h] = suffix_sums(z.pop((i, h)), diagonal=(tile_start and i == 0))

    @pl.when(qi >= FUSED_BLOCKS - 1)
    def _():
        sweep([qi - i for i in range(FUSED_BLOCKS)], tile_start=True, value_skew=VALUE_SKEW)

    @pl.when(qi < FUSED_BLOCKS - 1)
    def _():
        sweep([qi], tile_start=True, value_skew=VALUE_SKEW)

    row_groups = [slice(r, r + tq // TAIL_ROW_GROUPS) for r in range(0, tq, tq // TAIL_ROW_GROUPS)]

    def run_mins():
        return tuple(run_min(rows) for rows in row_groups)

    def more(carry):
        kb, mins = carry
        return jnp.logical_and(kb >= 0, functools.reduce(jnp.minimum, mins) < UNDERFLOW_LOG2)

    def body(carry):
        kb, mins = carry
        for rows, group_min in zip(row_groups, mins):
            @pl.when(group_min < UNDERFLOW_LOG2)
            def _():
                sweep([kb], tile_start=False, value_skew=n_heads, rows=rows)
        return kb - 1, run_mins()

    first_left = jnp.where(qi >= FUSED_BLOCKS - 1, qi - FUSED_BLOCKS, qi - 1)
    lax.while_loop(more, body, (first_left, run_mins()))

    first_head = lax.broadcasted_iota(jnp.int32, (tq, LANES), 1) < HEAD_DIM
    for h in range(0, n_heads, HEADS_PER_TILE):
        o_ref[0, :, lanes_of(h)] = jnp.where(first_head, acc_ref[h], acc_ref[h + 1]).astype(o_ref.dtype)


def _attention(q, k, v):
    b, s, w = k.shape
    width = ATTN_LANE_TILES * LANES
    n_heads = ATTN_LANE_TILES * HEADS_PER_TILE
    assert HEADS_PER_TILE == 2 and Q_TILE == K_TILE and s % Q_TILE == 0 and w % width == 0
    tile = pl.BlockSpec((1, Q_TILE, width), lambda i, h, j: (i, j, h))
    return pl.pallas_call(
        _attn_kernel,
        grid=(b, w // width, s // Q_TILE),
        in_specs=[pl.BlockSpec((1, HEADS_PER_TILE, Q_TILE, width), lambda i, h, j: (i, 0, j, h)),
                  tile, tile],
        out_specs=tile,
        out_shape=jax.ShapeDtypeStruct((b, s, w), _bf16),
        scratch_shapes=[
            pltpu.VMEM((n_heads, Q_TILE, LANES), _f32),
            pltpu.VMEM((n_heads, Q_TILE, LANES), _f32),
            pltpu.VMEM((s, width), _bf16),
            pltpu.VMEM((s, width), _bf16),
        ],
        compiler_params=pltpu.CompilerParams(
            dimension_semantics=("arbitrary", "arbitrary", "arbitrary"),
            vmem_limit_bytes=VMEM_LIMIT),
        name="attention",
    )(q, k, v)


def _mixer_out_kernel(x_ref, o_ref, gc_ref, sgb_ref, mod_ref, gains_ref,
                      wpa_ref, wout_ref, w1_ref, w2_ref, out_ref):
    d = x_ref.shape[-1]
    mod = lambda i: mod_ref[0, :, i * d:(i + 1) * d]
    gate1, shift2, scale2, gate2 = mod(2), mod(3), mod(4), mod(5)
    g_post_mix = gains_ref[0:1, :]
    g_pre_mlp = gains_ref[1:2, :]
    g_post_mlp = gains_ref[2:3, :]

    tm = x_ref.shape[1]
    subs = [pl.ds(r, tm // ROW_SUBTILES) for r in range(0, tm, tm // ROW_SUBTILES)]

    mix = []
    for rows in subs:
        y_attn = _dot(o_ref[0, rows, :], wpa_ref[...])
        merged = gc_ref[0, rows, :] + sgb_ref[0, rows, :] * y_attn
        mix.append(_dot(merged, wout_ref[...]))
    x1, h2 = [], []
    for i, rows in enumerate(subs):
        x1.append(x_ref[0, rows, :] + gate1 * _rms(mix[i], g_post_mix))
        h2.append((_rms(x1[i], g_pre_mlp) * (1.0 + scale2) + shift2).astype(_bf16))
    for i, rows in enumerate(subs):
        ff = jnp.zeros_like(x1[i])
        for j in range(0, w1_ref.shape[1], FF_CHUNK):
            hid = jnp.maximum(_dot(h2[i], w1_ref[:, j:j + FF_CHUNK]), 0.0)
            ff = ff + _dot((hid * hid).astype(_bf16), w2_ref[j:j + FF_CHUNK, :])
        out_ref[0, rows, :] = x1[i] + gate2 * _rms(ff, g_post_mlp)


def _mixer_out(layer, x, o, gc, sgb, mod, gains, w_proj_attn, w_out, w1, w2):
    b, s, d = x.shape
    tm = SEQ_TILE
    assert s % tm == 0 and tm % (ROW_SUBTILES * SUBLANES) == 0 and w1.shape[-1] % FF_CHUNK == 0
    tile = lambda width: pl.BlockSpec((1, tm, width), lambda i, j: (i, j, 0))
    return pl.pallas_call(
        _mixer_out_kernel,
        grid=(b, s // tm),
        in_specs=[
            tile(d), tile(ATTN_WIDTH), tile(d), tile(d),
            pl.BlockSpec((1, 1, N_MOD * d), lambda i, j: (i, 0, 0)),
            _const_spec(gains.shape),
            _layer_spec(w_proj_attn, layer),
            _layer_spec(w_out, layer),
            _const_spec(w1.shape),
            _const_spec(w2.shape),
        ],
        out_specs=tile(d),
        out_shape=jax.ShapeDtypeStruct((b, s, d), _f32),
        compiler_params=pltpu.CompilerParams(
            dimension_semantics=("arbitrary", "arbitrary"), vmem_limit_bytes=VMEM_LIMIT),
        name="mixer_out",
    )(x, o, gc, sgb, mod.reshape(b, 1, N_MOD * d), gains, w_proj_attn, w_out, w1, w2)


def kernel(x, c, w_ada, b_ada, g_pre_mix, g_post_mix, g_pre_mlp, g_post_mlp, w_in, conv_w,
           w_proj_conv, w_proj_attn, w_out, w_mlp_in, w_mlp_out):
    depth = w_ada.shape[0]
    mod = _ada(c, w_ada, b_ada)
    for l in range(depth):
        q, k, v, gc, sgb, w1, w2 = _mixer_in(l, x, mod[l], g_pre_mix[l], w_in, conv_w, w_proj_conv,
                                             w_mlp_in, w_mlp_out)
        o = _attention(q, k, v)
        gains = jnp.stack([g_post_mix[l], g_pre_mlp[l], g_post_mlp[l]])
        x = _mixer_out(l, x, o, gc, sgb, mod[l], gains, w_proj_attn, w_out, w1, w2)
    return x
```

```python
import functools
import math

import jax
import jax.numpy as jnp
from jax import lax
from jax.experimental import pallas as pl
from jax.experimental.pallas import tpu as pltpu

CONV_WIDTH = 512
CONV_TAPS = 3
ATTN_WIDTH = 512
HEAD_DIM = 64
N_MOD = 6
EPS = 1e-6

LANES = 128
SUBLANES = 8
HEADS_PER_TILE = LANES // HEAD_DIM
ATTN_LANE_TILES = 4
SEQ_TILE = 512
Q_TILE = 256
K_TILE = 256
ADA_COLS = 2048
FF_CHUNK = 1024
ROW_SUBTILES = 2
VMEM_LIMIT = 56 * 1024 * 1024
LOG2E = math.log2(math.e)
UNDERFLOW_LOG2 = 152.0
FUSED_BLOCKS = 3
WIDE_MARGIN_LOG2 = 100.0
HEAD_SKEW = 2
VALUE_SKEW = 4
TAIL_ROW_GROUPS = 2
MASKED_LOGIT = -1e30
Z_CLAMP = 120.0

_f32 = jnp.float32
_bf16 = jnp.bfloat16


def _dot(a, b):
    return jnp.dot(a, b, preferred_element_type=_f32)


def _rms(x, gain):
    return x * lax.rsqrt(jnp.mean(x * x, axis=-1, keepdims=True) + EPS) * gain


def _layer_spec(stacked, layer):
    rest = stacked.shape[1:]
    return pl.BlockSpec((None,) + rest, lambda *_: (layer,) + (0,) * len(rest),
                        pipeline_mode=pl.Buffered(1))


def _const_spec(shape):
    return pl.BlockSpec(shape, lambda *_: (0,) * len(shape), pipeline_mode=pl.Buffered(1))


def _ada_kernel(ct_ref, w_ref, b_ref, o_ref):
    w = w_ref[0]
    rows = [jnp.sum(w * ct_ref[:, i:i + 1], axis=0, keepdims=True) for i in range(ct_ref.shape[1])]
    o_ref[0] = jnp.concatenate(rows, axis=0) + b_ref[0, 0]


def _ada(c, w_ada, b_ada):
    depth, d, width = w_ada.shape
    b = c.shape[0]
    cols = ADA_COLS
    assert width % cols == 0
    return pl.pallas_call(
        _ada_kernel,
        grid=(depth, width // cols),
        in_specs=[
            pl.BlockSpec((d, b), lambda l, j: (0, 0)),
            pl.BlockSpec((1, d, cols), lambda l, j: (l, 0, j)),
            pl.BlockSpec((1, 1, 1, cols), lambda l, j: (l, j, 0, 0)),
        ],
        out_specs=pl.BlockSpec((1, b, cols), lambda l, j: (l, 0, j)),
        out_shape=jax.ShapeDtypeStruct((depth, b, width), _f32),
        compiler_params=pltpu.CompilerParams(
            dimension_semantics=("arbitrary", "arbitrary"), vmem_limit_bytes=VMEM_LIMIT),
        name="ada",
    )(c.T, w_ada, b_ada.reshape(depth, width // cols, 1, cols))


def _mixer_in_kernel(x_ref, mod_ref, g_ref, w_in_ref, cw_ref, wpc_ref, w1_ref, w2_ref,
                     q_ref, k_ref, v_ref, gc_ref, sgb_ref, w1_bf16_ref, w2_bf16_ref, carry_ref):
    d = x_ref.shape[-1]
    tm = x_ref.shape[1]
    cw = CONV_WIDTH

    w1_bf16_ref[...] = w1_ref[...].astype(_bf16)
    w2_bf16_ref[...] = w2_ref[...].astype(_bf16)

    @pl.when(pl.program_id(1) == 0)
    def _():
        carry_ref[...] = jnp.zeros_like(carry_ref)

    shift = mod_ref[0, :, 0:d]
    scale = mod_ref[0, :, d:2 * d]
    h = _rms(x_ref[0], g_ref[...]) * (1.0 + scale) + shift

    bg = _dot(h, w_in_ref[:, 0:cw])
    cg = _dot(h, w_in_ref[:, cw:2 * cw])
    u = _dot(h, w_in_ref[:, 2 * cw:3 * cw])

    o = 3 * cw
    aw = ATTN_WIDTH
    q = _dot(h, w_in_ref[:, o:o + aw]) * (LOG2E / math.sqrt(HEAD_DIM))
    head_in_tile = (lax.broadcasted_iota(jnp.int32, q.shape, 1) // HEAD_DIM) % HEADS_PER_TILE
    for p in range(HEADS_PER_TILE):
        q_ref[0, p] = jnp.where(head_in_tile == p, q, 0.0).astype(_bf16)
    k_ref[0] = _dot(h, w_in_ref[:, o + aw:o + 2 * aw]).astype(_bf16)
    v_ref[0] = _dot(h, w_in_ref[:, o + 2 * aw:o + 3 * aw]).astype(_bf16)
    o += 3 * aw
    gb = _dot(h, w_in_ref[:, o + d:o + 2 * d])
    sgb_ref[0] = jax.nn.sigmoid(gb)
    ga = _dot(h, w_in_ref[:, o:o + d])

    vv = cg * u
    row = lax.broadcasted_iota(jnp.int32, vv.shape, 0)
    prev1 = carry_ref[1:2, :]
    prev2 = carry_ref[0:1, :]
    v1 = jnp.where(row == 0, prev1, pltpu.roll(vv, 1, axis=0))
    v2 = jnp.where(row == 0, prev2, jnp.where(row == 1, prev1, pltpu.roll(vv, 2, axis=0)))
    carry_ref[...] = vv[tm - 2:tm, :]
    conv = cw_ref[0:1, :] * v2 + cw_ref[1:2, :] * v1 + cw_ref[2:3, :] * vv
    y_conv = _dot(bg * conv, wpc_ref[...])
    gc_ref[0] = jax.nn.sigmoid(ga) * y_conv


def _mixer_in(layer, x, mod, g_pre, w_in, conv_w, w_proj_conv, w1, w2):
    b, s, d = x.shape
    tm = SEQ_TILE
    assert s % tm == 0 and w_in.shape[-1] == 3 * CONV_WIDTH + 3 * ATTN_WIDTH + 2 * d
    assert conv_w.shape[1:] == (CONV_TAPS, CONV_WIDTH)
    steps = s // tm
    n_slices = b * steps
    bf16_rows = 2 * SUBLANES
    assert w1.shape[1] % (n_slices * bf16_rows) == 0 and w2.shape[1] % (n_slices * bf16_rows) == 0
    tile = lambda width: pl.BlockSpec((1, tm, width), lambda i, j: (i, j, 0))
    slice_in = lambda w: pl.BlockSpec((None, w.shape[1] // n_slices, w.shape[2]),
                                      lambda i, j: (layer, i * steps + j, 0))
    slice_out = lambda w: pl.BlockSpec((w.shape[1] // n_slices, w.shape[2]), lambda i, j: (i * steps + j, 0))
    return pl.pallas_call(
        _mixer_in_kernel,
        grid=(b, steps),
        in_specs=[
            tile(d),
            pl.BlockSpec((1, 1, N_MOD * d), lambda i, j: (i, 0, 0)),
            _const_spec((1, d)),
            _layer_spec(w_in, layer),
            _layer_spec(conv_w, layer),
            _layer_spec(w_proj_conv, layer),
            slice_in(w1), slice_in(w2),
        ],
        out_specs=[
            pl.BlockSpec((1, HEADS_PER_TILE, tm, ATTN_WIDTH), lambda i, j: (i, 0, j, 0)),
            tile(ATTN_WIDTH), tile(ATTN_WIDTH), tile(d), tile(d),
            slice_out(w1), slice_out(w2),
        ],
        out_shape=[
            jax.ShapeDtypeStruct((b, HEADS_PER_TILE, s, ATTN_WIDTH), _bf16),
            jax.ShapeDtypeStruct((b, s, ATTN_WIDTH), _bf16),
            jax.ShapeDtypeStruct((b, s, ATTN_WIDTH), _bf16),
            jax.ShapeDtypeStruct((b, s, d), _f32),
            jax.ShapeDtypeStruct((b, s, d), _f32),
            jax.ShapeDtypeStruct(w1.shape[1:], _bf16),
            jax.ShapeDtypeStruct(w2.shape[1:], _bf16),
        ],
        scratch_shapes=[pltpu.VMEM((CONV_TAPS - 1, CONV_WIDTH), _f32)],
        compiler_params=pltpu.CompilerParams(
            dimension_semantics=("arbitrary", "arbitrary"), vmem_limit_bytes=VMEM_LIMIT),
        name="mixer_in",
    )(x, mod.reshape(b, 1, N_MOD * d), g_pre.reshape(1, d), w_in, conv_w, w_proj_conv, w1, w2)


def _attn_kernel(q_ref, k_ref, v_ref, o_ref, acc_ref, run_ref, kbuf_ref, vbuf_ref, wide_ref):
    tq = q_ref.shape[2]
    qi = pl.program_id(2)
    n_heads = acc_ref.shape[0]
    lanes_of = lambda h: slice((h // HEADS_PER_TILE) * LANES, (h // HEADS_PER_TILE + 1) * LANES)

    tri2 = (lax.broadcasted_iota(jnp.int32, (K_TILE, K_TILE), 0)
            > lax.broadcasted_iota(jnp.int32, (K_TILE, K_TILE), 1)).astype(_bf16)

    heads = range(n_heads)

    def rows_of(kb):
        return pl.ds(pl.multiple_of(kb * K_TILE, K_TILE), K_TILE)

    every_row = slice(0, tq)

    kbuf_ref[rows_of(qi), :] = k_ref[0]
    vbuf_ref[rows_of(qi), :] = v_ref[0]

    def keys_values(kb, h, own_tile):
        if own_tile:
            return k_ref[0, :, lanes_of(h)], v_ref[0, :, lanes_of(h)]
        return kbuf_ref[rows_of(kb), lanes_of(h)], vbuf_ref[rows_of(kb), lanes_of(h)]

    def logits(kb, h, own_tile, rows=every_row):
        return lax.dot_general(q_ref[0, h % HEADS_PER_TILE, rows, lanes_of(h)],
                               keys_values(kb, h, own_tile)[0],
                               (((1,), (1,)), ((), ())), preferred_element_type=_f32)

    def suffix_sums(z, diagonal):
        sp = jnp.maximum(jnp.log2(1.0 + jnp.exp2(jnp.minimum(z, Z_CLAMP))), z)
        log_beta = z - sp
        if diagonal:
            keep = (lax.broadcasted_iota(jnp.int32, z.shape, 1)
                    < lax.broadcasted_iota(jnp.int32, z.shape, 0))
            sp = jnp.where(keep, sp, 0.0)
            log_beta = jnp.where(keep, log_beta, MASKED_LOGIT)
        return log_beta, (_dot(sp.astype(_bf16), tri2), sp[:, 0:1])

    def accumulate(kb, h, log_beta, sums, first, rows=every_row):
        right_of, leftmost = sums
        block_sum = jnp.broadcast_to(right_of[:, 0:1] + leftmost, (log_beta.shape[0], LANES))
        values = keys_values(kb, h, own_tile=first)[1]
        if first:
            acc_ref[h, rows, :] = _dot(jnp.exp2(log_beta - right_of).astype(_bf16), values)
            run_ref[h, rows, :] = block_sum
        else:
            run = run_ref[h, rows, :]
            a = jnp.exp2(log_beta - right_of - jnp.concatenate([run] * (K_TILE // LANES), axis=1))
            acc_ref[h, rows, :] += _dot(a.astype(_bf16), values)
            run_ref[h, rows, :] = run + block_sum

    def run_min(rows):
        parts = [run_ref[h, rows, :] for h in heads]
        while len(parts) > 1:
            parts = [jnp.minimum(a, b) for a, b in zip(parts[0::2], parts[1::2])]
        m = parts[0]
        while m.shape[0] > SUBLANES:
            half = m.shape[0] // 2
            m = jnp.minimum(m[:half], m[half:])
        return jnp.min(m)

    def sweep(blocks, tile_start, value_skew):
        work = [(i, h) for i in range(len(blocks)) for h in heads]
        z, staged = {}, {}
        for t in range(len(work) + value_skew):
            if 0 <= t - value_skew:
                i, h = work[t - value_skew]
                kb, rows = blocks[i]
                accumulate(kb, h, *staged.pop((i, h)), first=(tile_start and i == 0), rows=rows)
            if t < len(work):
                i, h = work[t]
                kb, rows = blocks[i]
                z[i, h] = logits(kb, h, own_tile=(tile_start and i == 0), rows=rows)
            if 0 <= t - HEAD_SKEW < len(work):
                i, h = work[t - HEAD_SKEW]
                staged[i, h] = suffix_sums(z.pop((i, h)), diagonal=(tile_start and i == 0))

    row_groups = [slice(r, r + tq // TAIL_ROW_GROUPS) for r in range(0, tq, tq // TAIL_ROW_GROUPS)]
    fused_wide = [(qi - i, every_row) for i in range(FUSED_BLOCKS)]
    fused_narrow = fused_wide[:-1] + [(qi - FUSED_BLOCKS + 1, row_groups[0])]

    @pl.when(qi == 0)
    def _():
        wide_ref[0] = 0

    fusable = qi >= FUSED_BLOCKS - 1
    wide = jnp.logical_and(fusable, wide_ref[0] == 1)
    narrow = jnp.logical_and(fusable, wide_ref[0] == 0)

    @pl.when(wide)
    def _():
        sweep(fused_wide, tile_start=True, value_skew=VALUE_SKEW)

    @pl.when(narrow)
    def _():
        sweep(fused_narrow, tile_start=True, value_skew=VALUE_SKEW)

    @pl.when(jnp.logical_not(fusable))
    def _():
        sweep([(qi, every_row)], tile_start=True, value_skew=VALUE_SKEW)

    def run_mins():
        return tuple(run_min(rows) for rows in row_groups)

    mins = run_mins()
    others_min = functools.reduce(jnp.minimum, mins[1:])

    @pl.when(narrow)
    def _():
        wide_ref[0] = (others_min < UNDERFLOW_LOG2).astype(jnp.int32)

    @pl.when(wide)
    def _():
        wide_ref[0] = (others_min < UNDERFLOW_LOG2 + WIDE_MARGIN_LOG2).astype(jnp.int32)

    def live(kb, group_min):
        return jnp.logical_and(kb >= 0, group_min < UNDERFLOW_LOG2)

    def more(carry):
        kbs, mins = carry
        return functools.reduce(jnp.logical_or, [live(kb, m) for kb, m in zip(kbs, mins)])

    def body(carry):
        kbs, mins = carry
        for rows, kb, group_min in zip(row_groups, kbs, mins):
            @pl.when(live(kb, group_min))
            def _():
                sweep([(kb, rows)], tile_start=False, value_skew=n_heads)
        return (tuple(jnp.where(live(kb, m), kb - 1, kb) for kb, m in zip(kbs, mins)), run_mins())

    after_fused = jnp.where(fusable, qi - FUSED_BLOCKS, qi - 1)
    after_narrow = jnp.where(narrow, after_fused + 1, after_fused)
    lax.while_loop(more, body, ((after_fused,) + (after_narrow,) * (TAIL_ROW_GROUPS - 1), mins))

    first_head = lax.broadcasted_iota(jnp.int32, (tq, LANES), 1) < HEAD_DIM
    for h in range(0, n_heads, HEADS_PER_TILE):
        o_ref[0, :, lanes_of(h)] = jnp.where(first_head, acc_ref[h], acc_ref[h + 1]).astype(o_ref.dtype)


def _attention(q, k, v):
    b, s, w = k.shape
    width = ATTN_LANE_TILES * LANES
    n_heads = ATTN_LANE_TILES * HEADS_PER_TILE
    assert HEADS_PER_TILE == 2 and Q_TILE == K_TILE and s % Q_TILE == 0 and w % width == 0
    tile = pl.BlockSpec((1, Q_TILE, width), lambda i, h, j: (i, j, h))
    return pl.pallas_call(
        _attn_kernel,
        grid=(b, w // width, s // Q_TILE),
        in_specs=[pl.BlockSpec((1, HEADS_PER_TILE, Q_TILE, width), lambda i, h, j: (i, 0, j, h)),
                  tile, tile],
        out_specs=tile,
        out_shape=jax.ShapeDtypeStruct((b, s, w), _bf16),
        scratch_shapes=[
            pltpu.VMEM((n_heads, Q_TILE, LANES), _f32),
            pltpu.VMEM((n_heads, Q_TILE, LANES), _f32),
            pltpu.VMEM((s, width), _bf16),
            pltpu.VMEM((s, width), _bf16),
            pltpu.SMEM((1,), jnp.int32),
        ],
        compiler_params=pltpu.CompilerParams(
            dimension_semantics=("arbitrary", "arbitrary", "arbitrary"),
            vmem_limit_bytes=VMEM_LIMIT),
        name="attention",
    )(q, k, v)


def _mixer_out_kernel(x_ref, o_ref, gc_ref, sgb_ref, mod_ref, gains_ref,
                      wpa_ref, wout_ref, w1_ref, w2_ref, out_ref):
    d = x_ref.shape[-1]
    mod = lambda i: mod_ref[0, :, i * d:(i + 1) * d]
    gate1, shift2, scale2, gate2 = mod(2), mod(3), mod(4), mod(5)
    g_post_mix = gains_ref[0:1, :]
    g_pre_mlp = gains_ref[1:2, :]
    g_post_mlp = gains_ref[2:3, :]

    tm = x_ref.shape[1]
    subs = [pl.ds(r, tm // ROW_SUBTILES) for r in range(0, tm, tm // ROW_SUBTILES)]

    mix = []
    for rows in subs:
        y_attn = _dot(o_ref[0, rows, :], wpa_ref[...])
        merged = gc_ref[0, rows, :] + sgb_ref[0, rows, :] * y_attn
        mix.append(_dot(merged, wout_ref[...]))
    x1, h2 = [], []
    for i, rows in enumerate(subs):
        x1.append(x_ref[0, rows, :] + gate1 * _rms(mix[i], g_post_mix))
        h2.append((_rms(x1[i], g_pre_mlp) * (1.0 + scale2) + shift2).astype(_bf16))
    for i, rows in enumerate(subs):
        ff = jnp.zeros_like(x1[i])
        for j in range(0, w1_ref.shape[1], FF_CHUNK):
            hid = jnp.maximum(_dot(h2[i], w1_ref[:, j:j + FF_CHUNK]), 0.0)
            ff = ff + _dot((hid * hid).astype(_bf16), w2_ref[j:j + FF_CHUNK, :])
        out_ref[0, rows, :] = x1[i] + gate2 * _rms(ff, g_post_mlp)


def _mixer_out(layer, x, o, gc, sgb, mod, gains, w_proj_attn, w_out, w1, w2):
    b, s, d = x.shape
    tm = SEQ_TILE
    assert s % tm == 0 and tm % (ROW_SUBTILES * SUBLANES) == 0 and w1.shape[-1] % FF_CHUNK == 0
    tile = lambda width: pl.BlockSpec((1, tm, width), lambda i, j: (i, j, 0))
    return pl.pallas_call(
        _mixer_out_kernel,
        grid=(b, s // tm),
        in_specs=[
            tile(d), tile(ATTN_WIDTH), tile(d), tile(d),
            pl.BlockSpec((1, 1, N_MOD * d), lambda i, j: (i, 0, 0)),
            _const_spec(gains.shape),
            _layer_spec(w_proj_attn, layer),
            _layer_spec(w_out, layer),
            _const_spec(w1.shape),
            _const_spec(w2.shape),
        ],
        out_specs=tile(d),
        out_shape=jax.ShapeDtypeStruct((b, s, d), _f32),
        compiler_params=pltpu.CompilerParams(
            dimension_semantics=("arbitrary", "arbitrary"), vmem_limit_bytes=VMEM_LIMIT),
        name="mixer_out",
    )(x, o, gc, sgb, mod.reshape(b, 1, N_MOD * d), gains, w_proj_attn, w_out, w1, w2)


def kernel(x, c, w_ada, b_ada, g_pre_mix, g_post_mix, g_pre_mlp, g_post_mlp, w_in, conv_w,
           w_proj_conv, w_proj_attn, w_out, w_mlp_in, w_mlp_out):
    depth = w_ada.shape[0]
    mod = _ada(c, w_ada, b_ada)
    for l in range(depth):
        q, k, v, gc, sgb, w1, w2 = _mixer_in(l, x, mod[l], g_pre_mix[l], w_in, conv_w, w_proj_conv,
                                             w_mlp_in, w_mlp_out)
        o = _attention(q, k, v)
        gains = jnp.stack([g_post_mix[l], g_pre_mlp[l], g_post_mlp[l]])
        x = _mixer_out(l, x, o, gc, sgb, mod[l], gains, w_proj_attn, w_out, w1, w2)
    return x
```

```python
import functools
import math

import jax
import jax.numpy as jnp
from jax import lax
from jax.experimental import pallas as pl
from jax.experimental.pallas import tpu as pltpu

CONV_WIDTH = 512
CONV_TAPS = 3
ATTN_WIDTH = 512
HEAD_DIM = 64
N_MOD = 6
EPS = 1e-6

LANES = 128
SUBLANES = 8
HEADS_PER_TILE = LANES // HEAD_DIM
ATTN_LANE_TILES = 4
SEQ_TILE = 512
Q_TILE = 256
K_TILE = 256
ADA_COLS = 2048
FF_CHUNK = 1024
ROW_SUBTILES = 2
VMEM_LIMIT = 56 * 1024 * 1024
LOG2E = math.log2(math.e)
UNDERFLOW_LOG2 = 152.0
FUSED_BLOCKS = 3
HEAD_SKEW = 2
VALUE_SKEW = 4
TAIL_ROW_GROUPS = 2
MASKED_LOGIT = -1e30
Z_CLAMP = 120.0

_f32 = jnp.float32
_bf16 = jnp.bfloat16


def _dot(a, b):
    return jnp.dot(a, b, preferred_element_type=_f32)


def _rms(x, gain):
    return x * lax.rsqrt(jnp.mean(x * x, axis=-1, keepdims=True) + EPS) * gain


def _layer_spec(stacked, layer):
    rest = stacked.shape[1:]
    return pl.BlockSpec((None,) + rest, lambda *_: (layer,) + (0,) * len(rest),
                        pipeline_mode=pl.Buffered(1))


def _const_spec(shape):
    return pl.BlockSpec(shape, lambda *_: (0,) * len(shape), pipeline_mode=pl.Buffered(1))


def _ada_kernel(ct_ref, w_ref, b_ref, o_ref):
    w = w_ref[0]
    rows = [jnp.sum(w * ct_ref[:, i:i + 1], axis=0, keepdims=True) for i in range(ct_ref.shape[1])]
    o_ref[0] = jnp.concatenate(rows, axis=0) + b_ref[0, 0]


def _ada(c, w_ada, b_ada):
    depth, d, width = w_ada.shape
    b = c.shape[0]
    cols = ADA_COLS
    assert width % cols == 0
    return pl.pallas_call(
        _ada_kernel,
        grid=(depth, width // cols),
        in_specs=[
            pl.BlockSpec((d, b), lambda l, j: (0, 0)),
            pl.BlockSpec((1, d, cols), lambda l, j: (l, 0, j)),
            pl.BlockSpec((1, 1, 1, cols), lambda l, j: (l, j, 0, 0)),
        ],
        out_specs=pl.BlockSpec((1, b, cols), lambda l, j: (l, 0, j)),
        out_shape=jax.ShapeDtypeStruct((depth, b, width), _f32),
        compiler_params=pltpu.CompilerParams(
            dimension_semantics=("arbitrary", "arbitrary"), vmem_limit_bytes=VMEM_LIMIT),
        name="ada",
    )(c.T, w_ada, b_ada.reshape(depth, width // cols, 1, cols))


def _mixer_in_kernel(x_ref, x_next_ref, mod_ref, g_ref, w_in_ref, cw_ref, wpc_ref, w1_ref, w2_ref,
                     q_ref, k_ref, v_ref, gc_ref, sgb_ref, w1_bf16_ref, w2_bf16_ref,
                     carry_ref, h_even_ref, h_odd_ref, bg_even_ref, bg_odd_ref):
    d = x_ref.shape[-1]
    cw = CONV_WIDTH
    j = pl.program_id(1)

    w1_bf16_ref[...] = w1_ref[...].astype(_bf16)
    w2_bf16_ref[...] = w2_ref[...].astype(_bf16)

    def prepare(tile_ref, h_ref, bg_ref):
        shift = mod_ref[0, :, 0:d]
        scale = mod_ref[0, :, d:2 * d]
        h = _rms(tile_ref[0], g_ref[...]) * (1.0 + scale) + shift
        h_ref[...] = h.astype(_bf16)
        bg_ref[...] = _dot(h, w_in_ref[:, 0:cw])

    @pl.when(j == 0)
    def _():
        carry_ref[...] = jnp.zeros_like(carry_ref)
        prepare(x_ref, h_even_ref, bg_even_ref)

    step = functools.partial(_mixer_in_step, w_in_ref, cw_ref, wpc_ref, q_ref, k_ref, v_ref,
                             gc_ref, sgb_ref, carry_ref)

    @pl.when(j % 2 == 0)
    def _():
        step(h_even_ref, bg_even_ref, functools.partial(prepare, x_next_ref, h_odd_ref, bg_odd_ref))

    @pl.when(j % 2 == 1)
    def _():
        step(h_odd_ref, bg_odd_ref, functools.partial(prepare, x_next_ref, h_even_ref, bg_even_ref))


def _mixer_in_step(w_in_ref, cw_ref, wpc_ref, q_ref, k_ref, v_ref, gc_ref, sgb_ref, carry_ref,
                   h_ref, bg_ref, prepare_next):
    tm, d = h_ref.shape
    cw = CONV_WIDTH
    h = h_ref[...].astype(_f32)

    cg = _dot(h, w_in_ref[:, cw:2 * cw])
    u = _dot(h, w_in_ref[:, 2 * cw:3 * cw])

    o = 3 * cw
    aw = ATTN_WIDTH
    q = _dot(h, w_in_ref[:, o:o + aw]) * (LOG2E / math.sqrt(HEAD_DIM))
    head_in_tile = (lax.broadcasted_iota(jnp.int32, q.shape, 1) // HEAD_DIM) % HEADS_PER_TILE
    for p in range(HEADS_PER_TILE):
        q_ref[0, p] = jnp.where(head_in_tile == p, q, 0.0).astype(_bf16)
    k_ref[0] = _dot(h, w_in_ref[:, o + aw:o + 2 * aw]).astype(_bf16)
    v_ref[0] = _dot(h, w_in_ref[:, o + 2 * aw:o + 3 * aw]).astype(_bf16)
    o += 3 * aw
    gb = _dot(h, w_in_ref[:, o + d:o + 2 * d])
    sgb_ref[0] = jax.nn.sigmoid(gb)
    ga = _dot(h, w_in_ref[:, o:o + d])
    prepare_next()

    bg = bg_ref[...]
    vv = cg * u
    row = lax.broadcasted_iota(jnp.int32, vv.shape, 0)
    prev1 = carry_ref[1:2, :]
    prev2 = carry_ref[0:1, :]
    v1 = jnp.where(row == 0, prev1, pltpu.roll(vv, 1, axis=0))
    v2 = jnp.where(row == 0, prev2, jnp.where(row == 1, prev1, pltpu.roll(vv, 2, axis=0)))
    carry_ref[...] = vv[tm - 2:tm, :]
    conv = cw_ref[0:1, :] * v2 + cw_ref[1:2, :] * v1 + cw_ref[2:3, :] * vv
    y_conv = _dot(bg * conv, wpc_ref[...])
    gc_ref[0] = jax.nn.sigmoid(ga) * y_conv


def _mixer_in(layer, x, mod, g_pre, w_in, conv_w, w_proj_conv, w1, w2):
    b, s, d = x.shape
    tm = SEQ_TILE
    assert s % tm == 0 and w_in.shape[-1] == 3 * CONV_WIDTH + 3 * ATTN_WIDTH + 2 * d
    assert conv_w.shape[1:] == (CONV_TAPS, CONV_WIDTH)
    steps = s // tm
    n_slices = b * steps
    bf16_rows = 2 * SUBLANES
    assert w1.shape[1] % (n_slices * bf16_rows) == 0 and w2.shape[1] % (n_slices * bf16_rows) == 0
    tile = lambda width: pl.BlockSpec((1, tm, width), lambda i, j: (i, j, 0))
    slice_in = lambda w: pl.BlockSpec((None, w.shape[1] // n_slices, w.shape[2]),
                                      lambda i, j: (layer, i * steps + j, 0))
    slice_out = lambda w: pl.BlockSpec((w.shape[1] // n_slices, w.shape[2]), lambda i, j: (i * steps + j, 0))
    return pl.pallas_call(
        _mixer_in_kernel,
        grid=(b, steps),
        in_specs=[
            tile(d),
            pl.BlockSpec((1, tm, d), lambda i, j: (i, jnp.minimum(j + 1, steps - 1), 0)),
            pl.BlockSpec((1, 1, N_MOD * d), lambda i, j: (i, 0, 0)),
            _const_spec((1, d)),
            _layer_spec(w_in, layer),
            _layer_spec(conv_w, layer),
            _layer_spec(w_proj_conv, layer),
            slice_in(w1), slice_in(w2),
        ],
        out_specs=[
            pl.BlockSpec((1, HEADS_PER_TILE, tm, ATTN_WIDTH), lambda i, j: (i, 0, j, 0)),
            tile(ATTN_WIDTH), tile(ATTN_WIDTH), tile(d), tile(d),
            slice_out(w1), slice_out(w2),
        ],
        out_shape=[
            jax.ShapeDtypeStruct((b, HEADS_PER_TILE, s, ATTN_WIDTH), _bf16),
            jax.ShapeDtypeStruct((b, s, ATTN_WIDTH), _bf16),
            jax.ShapeDtypeStruct((b, s, ATTN_WIDTH), _bf16),
            jax.ShapeDtypeStruct((b, s, d), _f32),
            jax.ShapeDtypeStruct((b, s, d), _f32),
            jax.ShapeDtypeStruct(w1.shape[1:], _bf16),
            jax.ShapeDtypeStruct(w2.shape[1:], _bf16),
        ],
        scratch_shapes=[
            pltpu.VMEM((CONV_TAPS - 1, CONV_WIDTH), _f32),
            pltpu.VMEM((tm, d), _bf16), pltpu.VMEM((tm, d), _bf16),
            pltpu.VMEM((tm, CONV_WIDTH), _f32), pltpu.VMEM((tm, CONV_WIDTH), _f32),
        ],
        compiler_params=pltpu.CompilerParams(
            dimension_semantics=("arbitrary", "arbitrary"), vmem_limit_bytes=VMEM_LIMIT),
        name="mixer_in",
    )(x, x, mod.reshape(b, 1, N_MOD * d), g_pre.reshape(1, d), w_in, conv_w, w_proj_conv, w1, w2)


def _attn_kernel(q_ref, k_ref, v_ref, o_ref, acc_ref, run_ref, kbuf_ref, vbuf_ref):
    tq = q_ref.shape[2]
    qi = pl.program_id(2)
    n_heads = acc_ref.shape[0]
    lanes_of = lambda h: slice((h // HEADS_PER_TILE) * LANES, (h // HEADS_PER_TILE + 1) * LANES)

    tri2 = (lax.broadcasted_iota(jnp.int32, (K_TILE, K_TILE), 0)
            > lax.broadcasted_iota(jnp.int32, (K_TILE, K_TILE), 1)).astype(_bf16)

    heads = range(n_heads)

    def rows_of(kb):
        return pl.ds(pl.multiple_of(kb * K_TILE, K_TILE), K_TILE)

    every_row = slice(0, tq)

    kbuf_ref[rows_of(qi), :] = k_ref[0]
    vbuf_ref[rows_of(qi), :] = v_ref[0]

    def keys_values(kb, h, own_tile):
        if own_tile:
            return k_ref[0, :, lanes_of(h)], v_ref[0, :, lanes_of(h)]
        return kbuf_ref[rows_of(kb), lanes_of(h)], vbuf_ref[rows_of(kb), lanes_of(h)]

    def logits(kb, h, own_tile, rows=every_row):
        return lax.dot_general(q_ref[0, h % HEADS_PER_TILE, rows, lanes_of(h)],
                               keys_values(kb, h, own_tile)[0],
                               (((1,), (1,)), ((), ())), preferred_element_type=_f32)

    def suffix_sums(z, diagonal):
        sp = jnp.maximum(jnp.log2(1.0 + jnp.exp2(jnp.minimum(z, Z_CLAMP))), z)
        log_beta = z - sp
        if diagonal:
            keep = (lax.broadcasted_iota(jnp.int32, z.shape, 1)
                    < lax.broadcasted_iota(jnp.int32, z.shape, 0))
            sp = jnp.where(keep, sp, 0.0)
            log_beta = jnp.where(keep, log_beta, MASKED_LOGIT)
        return log_beta, (_dot(sp.astype(_bf16), tri2), sp[:, 0:1])

    def accumulate(kb, h, log_beta, sums, first, rows=every_row):
        right_of, leftmost = sums
        block_sum = jnp.broadcast_to(right_of[:, 0:1] + leftmost, (log_beta.shape[0], LANES))
        values = keys_values(kb, h, own_tile=first)[1]
        if first:
            acc_ref[h, rows, :] = _dot(jnp.exp2(log_beta - right_of).astype(_bf16), values)
            run_ref[h, rows, :] = block_sum
        else:
            run = run_ref[h, rows, :]
            a = jnp.exp2(log_beta - right_of - jnp.concatenate([run] * (K_TILE // LANES), axis=1))
            acc_ref[h, rows, :] += _dot(a.astype(_bf16), values)
            run_ref[h, rows, :] = run + block_sum

    def run_min(rows):
        parts = [run_ref[h, rows, :] for h in heads]
        while len(parts) > 1:
            parts = [jnp.minimum(a, b) for a, b in zip(parts[0::2], parts[1::2])]
        m = parts[0]
        while m.shape[0] > SUBLANES:
            half = m.shape[0] // 2
            m = jnp.minimum(m[:half], m[half:])
        return jnp.min(m)

    def sweep(blocks, tile_start, value_skew, rows=every_row):
        work = [(i, h) for i in range(len(blocks)) for h in heads]
        z, staged = {}, {}
        for t in range(len(work) + value_skew):
            if 0 <= t - value_skew:
                i, h = work[t - value_skew]
                accumulate(blocks[i], h, *staged.pop((i, h)), first=(tile_start and i == 0), rows=rows)
            if t < len(work):
                i, h = work[t]
                z[i, h] = logits(blocks[i], h, own_tile=(tile_start and i == 0), rows=rows)
            if 0 <= t - HEAD_SKEW < len(work):
                i, h = work[t - HEAD_SKEW]
                staged[i, h] = suffix_sums(z.pop((i, h)), diagonal=(tile_start and i == 0))

    @pl.when(qi >= FUSED_BLOCKS - 1)
    def _():
        sweep([qi - i for i in range(FUSED_BLOCKS)], tile_start=True, value_skew=VALUE_SKEW)

    @pl.when(qi < FUSED_BLOCKS - 1)
    def _():
        sweep([qi], tile_start=True, value_skew=VALUE_SKEW)

    row_groups = [slice(r, r + tq // TAIL_ROW_GROUPS) for r in range(0, tq, tq // TAIL_ROW_GROUPS)]

    def run_mins():
        return tuple(run_min(rows) for rows in row_groups)

    def more(carry):
        kb, mins = carry
        return jnp.logical_and(kb >= 0, functools.reduce(jnp.minimum, mins) < UNDERFLOW_LOG2)

    def body(carry):
        kb, mins = carry
        for rows, group_min in zip(row_groups, mins):
            @pl.when(group_min < UNDERFLOW_LOG2)
            def _():
                sweep([kb], tile_start=False, value_skew=n_heads, rows=rows)
        return kb - 1, run_mins()

    first_left = jnp.where(qi >= FUSED_BLOCKS - 1, qi - FUSED_BLOCKS, qi - 1)
    lax.while_loop(more, body, (first_left, run_mins()))

    first_head = lax.broadcasted_iota(jnp.int32, (tq, LANES), 1) < HEAD_DIM
    for h in range(0, n_heads, HEADS_PER_TILE):
        o_ref[0, :, lanes_of(h)] = jnp.where(first_head, acc_ref[h], acc_ref[h + 1]).astype(o_ref.dtype)


def _attention(q, k, v):
    b, s, w = k.shape
    width = ATTN_LANE_TILES * LANES
    n_heads = ATTN_LANE_TILES * HEADS_PER_TILE
    assert HEADS_PER_TILE == 2 and Q_TILE == K_TILE and s % Q_TILE == 0 and w % width == 0
    tile = pl.BlockSpec((1, Q_TILE, width), lambda i, h, j: (i, j, h))
    return pl.pallas_call(
        _attn_kernel,
        grid=(b, w // width, s // Q_TILE),
        in_specs=[pl.BlockSpec((1, HEADS_PER_TILE, Q_TILE, width), lambda i, h, j: (i, 0, j, h)),
                  tile, tile],
        out_specs=tile,
        out_shape=jax.ShapeDtypeStruct((b, s, w), _bf16),
        scratch_shapes=[
            pltpu.VMEM((n_heads, Q_TILE, LANES), _f32),
            pltpu.VMEM((n_heads, Q_TILE, LANES), _f32),
            pltpu.VMEM((s, width), _bf16),
            pltpu.VMEM((s, width), _bf16),
        ],
        compiler_params=pltpu.CompilerParams(
            dimension_semantics=("arbitrary", "arbitrary", "arbitrary"),
            vmem_limit_bytes=VMEM_LIMIT),
        name="attention",
    )(q, k, v)


def _mixer_out_kernel(x_ref, o_ref, gc_ref, sgb_ref, mod_ref, gains_ref,
                      wpa_ref, wout_ref, w1_ref, w2_ref, out_ref):
    d = x_ref.shape[-1]
    mod = lambda i: mod_ref[0, :, i * d:(i + 1) * d]
    gate1, shift2, scale2, gate2 = mod(2), mod(3), mod(4), mod(5)
    g_post_mix = gains_ref[0:1, :]
    g_pre_mlp = gains_ref[1:2, :]
    g_post_mlp = gains_ref[2:3, :]

    tm = x_ref.shape[1]
    subs = [pl.ds(r, tm // ROW_SUBTILES) for r in range(0, tm, tm // ROW_SUBTILES)]

    mix = []
    for rows in subs:
        y_attn = _dot(o_ref[0, rows, :], wpa_ref[...])
        merged = gc_ref[0, rows, :] + sgb_ref[0, rows, :] * y_attn
        mix.append(_dot(merged, wout_ref[...]))
    x1, h2 = [], []
    for i, rows in enumerate(subs):
        x1.append(x_ref[0, rows, :] + gate1 * _rms(mix[i], g_post_mix))
        h2.append((_rms(x1[i], g_pre_mlp) * (1.0 + scale2) + shift2).astype(_bf16))
    for i, rows in enumerate(subs):
        ff = jnp.zeros_like(x1[i])
        for j in range(0, w1_ref.shape[1], FF_CHUNK):
            hid = jnp.maximum(_dot(h2[i], w1_ref[:, j:j + FF_CHUNK]), 0.0)
            ff = ff + _dot((hid * hid).astype(_bf16), w2_ref[j:j + FF_CHUNK, :])
        out_ref[0, rows, :] = x1[i] + gate2 * _rms(ff, g_post_mlp)


def _mixer_out(layer, x, o, gc, sgb, mod, gains, w_proj_attn, w_out, w1, w2):
    b, s, d = x.shape
    tm = SEQ_TILE
    assert s % tm == 0 and tm % (ROW_SUBTILES * SUBLANES) == 0 and w1.shape[-1] % FF_CHUNK == 0
    tile = lambda width: pl.BlockSpec((1, tm, width), lambda i, j: (i, j, 0))
    return pl.pallas_call(
        _mixer_out_kernel,
        grid=(b, s // tm),
        in_specs=[
            tile(d), tile(ATTN_WIDTH), tile(d), tile(d),
            pl.BlockSpec((1, 1, N_MOD * d), lambda i, j: (i, 0, 0)),
            _const_spec(gains.shape),
            _layer_spec(w_proj_attn, layer),
            _layer_spec(w_out, layer),
            _const_spec(w1.shape),
            _const_spec(w2.shape),
        ],
        out_specs=tile(d),
        out_shape=jax.ShapeDtypeStruct((b, s, d), _f32),
        compiler_params=pltpu.CompilerParams(
            dimension_semantics=("arbitrary", "arbitrary"), vmem_limit_bytes=VMEM_LIMIT),
        name="mixer_out",
    )(x, o, gc, sgb, mod.reshape(b, 1, N_MOD * d), gains, w_proj_attn, w_out, w1, w2)


def kernel(x, c, w_ada, b_ada, g_pre_mix, g_post_mix, g_pre_mlp, g_post_mlp, w_in, conv_w,
           w_proj_conv, w_proj_attn, w_out, w_mlp_in, w_mlp_out):
    depth = w_ada.shape[0]
    mod = _ada(c, w_ada, b_ada)
    for l in range(depth):
        q, k, v, gc, sgb, w1, w2 = _mixer_in(l, x, mod[l], g_pre_mix[l], w_in, conv_w, w_proj_conv,
                                             w_mlp_in, w_mlp_out)
        o = _attention(q, k, v)
        gains = jnp.stack([g_post_mix[l], g_pre_mlp[l], g_post_mlp[l]])
        x = _mixer_out(l, x, o, gc, sgb, mod[l], gains, w_proj_attn, w_out, w1, w2)
    return x
```

```python
import functools
import math

import jax
import jax.numpy as jnp
from jax import lax
from jax.experimental import pallas as pl
from jax.experimental.pallas import tpu as pltpu

CONV_WIDTH = 512
CONV_TAPS = 3
ATTN_WIDTH = 512
HEAD_DIM = 64
N_MOD = 6
EPS = 1e-6

LANES = 128
SUBLANES = 8
HEADS_PER_TILE = LANES // HEAD_DIM
ATTN_LANE_TILES = 4
SEQ_TILE = 512
Q_TILE = 256
K_TILE = 256
ADA_COLS = 2048
FF_CHUNK = 1024
ROW_SUBTILES = 2
VMEM_LIMIT = 56 * 1024 * 1024
LOG2E = math.log2(math.e)
UNDERFLOW_LOG2 = 152.0
FUSED_BLOCKS = 3
HEAD_SKEW = 2
VALUE_SKEW = 4
TAIL_ROW_GROUPS = 2
MASKED_LOGIT = -1e30
Z_CLAMP = 120.0

_f32 = jnp.float32
_bf16 = jnp.bfloat16


def _dot(a, b):
    return jnp.dot(a, b, preferred_element_type=_f32)


def _rms(x, gain):
    return x * lax.rsqrt(jnp.mean(x * x, axis=-1, keepdims=True) + EPS) * gain


def _layer_spec(stacked, layer):
    rest = stacked.shape[1:]
    return pl.BlockSpec((None,) + rest, lambda *_: (layer,) + (0,) * len(rest),
                        pipeline_mode=pl.Buffered(1))


def _const_spec(shape):
    return pl.BlockSpec(shape, lambda *_: (0,) * len(shape), pipeline_mode=pl.Buffered(1))


def _ada_kernel(ct_ref, w_ref, b_ref, o_ref):
    w = w_ref[0]
    rows = [jnp.sum(w * ct_ref[:, i:i + 1], axis=0, keepdims=True) for i in range(ct_ref.shape[1])]
    o_ref[0] = jnp.concatenate(rows, axis=0) + b_ref[0, 0]


def _ada(c, w_ada, b_ada):
    depth, d, width = w_ada.shape
    b = c.shape[0]
    cols = ADA_COLS
    assert width % cols == 0
    return pl.pallas_call(
        _ada_kernel,
        grid=(depth, width // cols),
        in_specs=[
            pl.BlockSpec((d, b), lambda l, j: (0, 0)),
            pl.BlockSpec((1, d, cols), lambda l, j: (l, 0, j)),
            pl.BlockSpec((1, 1, 1, cols), lambda l, j: (l, j, 0, 0)),
        ],
        out_specs=pl.BlockSpec((1, b, cols), lambda l, j: (l, 0, j)),
        out_shape=jax.ShapeDtypeStruct((depth, b, width), _f32),
        compiler_params=pltpu.CompilerParams(
            dimension_semantics=("arbitrary", "arbitrary"), vmem_limit_bytes=VMEM_LIMIT),
        name="ada",
    )(c.T, w_ada, b_ada.reshape(depth, width // cols, 1, cols))


def _mixer_in_kernel(x_ref, x_next_ref, mod_ref, g_ref, w_in_ref, cw_ref, wpc_ref, w1_ref, w2_ref,
                     q_ref, k_ref, v_ref, gc_ref, sgb_ref, w1_bf16_ref, w2_bf16_ref,
                     carry_ref, h_ref, bg_ref):
    d = x_ref.shape[-1]
    tm = x_ref.shape[1]
    cw = CONV_WIDTH

    w1_bf16_ref[...] = w1_ref[...].astype(_bf16)
    w2_bf16_ref[...] = w2_ref[...].astype(_bf16)

    def prepare(tile_ref):
        shift = mod_ref[0, :, 0:d]
        scale = mod_ref[0, :, d:2 * d]
        h = _rms(tile_ref[0], g_ref[...]) * (1.0 + scale) + shift
        h_ref[...] = h.astype(_bf16)
        bg_ref[...] = _dot(h, w_in_ref[:, 0:cw])

    @pl.when(pl.program_id(1) == 0)
    def _():
        carry_ref[...] = jnp.zeros_like(carry_ref)
        prepare(x_ref)

    h = h_ref[...].astype(_f32)

    cg = _dot(h, w_in_ref[:, cw:2 * cw])
    u = _dot(h, w_in_ref[:, 2 * cw:3 * cw])

    o = 3 * cw
    aw = ATTN_WIDTH
    q = _dot(h, w_in_ref[:, o:o + aw]) * (LOG2E / math.sqrt(HEAD_DIM))
    head_in_tile = (lax.broadcasted_iota(jnp.int32, q.shape, 1) // HEAD_DIM) % HEADS_PER_TILE
    for p in range(HEADS_PER_TILE):
        q_ref[0, p] = jnp.where(head_in_tile == p, q, 0.0).astype(_bf16)
    k_ref[0] = _dot(h, w_in_ref[:, o + aw:o + 2 * aw]).astype(_bf16)
    v_ref[0] = _dot(h, w_in_ref[:, o + 2 * aw:o + 3 * aw]).astype(_bf16)
    o += 3 * aw
    gb = _dot(h, w_in_ref[:, o + d:o + 2 * d])
    sgb_ref[0] = jax.nn.sigmoid(gb)
    ga = _dot(h, w_in_ref[:, o:o + d])

    bg = bg_ref[...]
    prepare(x_next_ref)

    vv = cg * u
    row = lax.broadcasted_iota(jnp.int32, vv.shape, 0)
    prev1 = carry_ref[1:2, :]
    prev2 = carry_ref[0:1, :]
    v1 = jnp.where(row == 0, prev1, pltpu.roll(vv, 1, axis=0))
    v2 = jnp.where(row == 0, prev2, jnp.where(row == 1, prev1, pltpu.roll(vv, 2, axis=0)))
    carry_ref[...] = vv[tm - 2:tm, :]
    conv = cw_ref[0:1, :] * v2 + cw_ref[1:2, :] * v1 + cw_ref[2:3, :] * vv
    y_conv = _dot(bg * conv, wpc_ref[...])
    gc_ref[0] = jax.nn.sigmoid(ga) * y_conv


def _mixer_in(layer, x, mod, g_pre, w_in, conv_w, w_proj_conv, w1, w2):
    b, s, d = x.shape
    tm = SEQ_TILE
    assert s % tm == 0 and w_in.shape[-1] == 3 * CONV_WIDTH + 3 * ATTN_WIDTH + 2 * d
    assert conv_w.shape[1:] == (CONV_TAPS, CONV_WIDTH)
    steps = s // tm
    n_slices = b * steps
    bf16_rows = 2 * SUBLANES
    assert w1.shape[1] % (n_slices * bf16_rows) == 0 and w2.shape[1] % (n_slices * bf16_rows) == 0
    tile = lambda width: pl.BlockSpec((1, tm, width), lambda i, j: (i, j, 0))
    slice_in = lambda w: pl.BlockSpec((None, w.shape[1] // n_slices, w.shape[2]),
                                      lambda i, j: (layer, i * steps + j, 0))
    slice_out = lambda w: pl.BlockSpec((w.shape[1] // n_slices, w.shape[2]), lambda i, j: (i * steps + j, 0))
    return pl.pallas_call(
        _mixer_in_kernel,
        grid=(b, steps),
        in_specs=[
            tile(d),
            pl.BlockSpec((1, tm, d), lambda i, j: (i, jnp.minimum(j + 1, steps - 1), 0)),
            pl.BlockSpec((1, 1, N_MOD * d), lambda i, j: (i, 0, 0)),
            _const_spec((1, d)),
            _layer_spec(w_in, layer),
            _layer_spec(conv_w, layer),
            _layer_spec(w_proj_conv, layer),
            slice_in(w1), slice_in(w2),
        ],
        out_specs=[
            pl.BlockSpec((1, HEADS_PER_TILE, tm, ATTN_WIDTH), lambda i, j: (i, 0, j, 0)),
            tile(ATTN_WIDTH), tile(ATTN_WIDTH), tile(d), tile(d),
            slice_out(w1), slice_out(w2),
        ],
        out_shape=[
            jax.ShapeDtypeStruct((b, HEADS_PER_TILE, s, ATTN_WIDTH), _bf16),
            jax.ShapeDtypeStruct((b, s, ATTN_WIDTH), _bf16),
            jax.ShapeDtypeStruct((b, s, ATTN_WIDTH), _bf16),
            jax.ShapeDtypeStruct((b, s, d), _f32),
            jax.ShapeDtypeStruct((b, s, d), _f32),
            jax.ShapeDtypeStruct(w1.shape[1:], _bf16),
            jax.ShapeDtypeStruct(w2.shape[1:], _bf16),
        ],
        scratch_shapes=[
            pltpu.VMEM((CONV_TAPS - 1, CONV_WIDTH), _f32),
            pltpu.VMEM((tm, d), _bf16),
            pltpu.VMEM((tm, CONV_WIDTH), _f32),
        ],
        compiler_params=pltpu.CompilerParams(
            dimension_semantics=("arbitrary", "arbitrary"), vmem_limit_bytes=VMEM_LIMIT),
        name="mixer_in",
    )(x, x, mod.reshape(b, 1, N_MOD * d), g_pre.reshape(1, d), w_in, conv_w, w_proj_conv, w1, w2)


def _attn_kernel(q_ref, k_ref, v_ref, o_ref, acc_ref, run_ref, kbuf_ref, vbuf_ref):
    tq = q_ref.shape[2]
    qi = pl.program_id(2)
    n_heads = acc_ref.shape[0]
    lanes_of = lambda h: slice((h // HEADS_PER_TILE) * LANES, (h // HEADS_PER_TILE + 1) * LANES)

    tri2 = (lax.broadcasted_iota(jnp.int32, (K_TILE, K_TILE), 0)
            > lax.broadcasted_iota(jnp.int32, (K_TILE, K_TILE), 1)).astype(_bf16)

    heads = range(n_heads)

    def rows_of(kb):
        return pl.ds(pl.multiple_of(kb * K_TILE, K_TILE), K_TILE)

    every_row = slice(0, tq)

    kbuf_ref[rows_of(qi), :] = k_ref[0]
    vbuf_ref[rows_of(qi), :] = v_ref[0]

    def keys_values(kb, h, own_tile):
        if own_tile:
            return k_ref[0, :, lanes_of(h)], v_ref[0, :, lanes_of(h)]
        return kbuf_ref[rows_of(kb), lanes_of(h)], vbuf_ref[rows_of(kb), lanes_of(h)]

    def logits(kb, h, own_tile, rows=every_row):
        return lax.dot_general(q_ref[0, h % HEADS_PER_TILE, rows, lanes_of(h)],
                               keys_values(kb, h, own_tile)[0],
                               (((1,), (1,)), ((), ())), preferred_element_type=_f32)

    def suffix_sums(z, diagonal):
        sp = jnp.maximum(jnp.log2(1.0 + jnp.exp2(jnp.minimum(z, Z_CLAMP))), z)
        log_beta = z - sp
        if diagonal:
            keep = (lax.broadcasted_iota(jnp.int32, z.shape, 1)
                    < lax.broadcasted_iota(jnp.int32, z.shape, 0))
            sp = jnp.where(keep, sp, 0.0)
            log_beta = jnp.where(keep, log_beta, MASKED_LOGIT)
        return log_beta, (_dot(sp.astype(_bf16), tri2), sp[:, 0:1])

    def accumulate(kb, h, log_beta, sums, first, rows=every_row):
        right_of, leftmost = sums
        block_sum = jnp.broadcast_to(right_of[:, 0:1] + leftmost, (log_beta.shape[0], LANES))
        values = keys_values(kb, h, own_tile=first)[1]
        if first:
            acc_ref[h, rows, :] = _dot(jnp.exp2(log_beta - right_of).astype(_bf16), values)
            run_ref[h, rows, :] = block_sum
        else:
            run = run_ref[h, rows, :]
            a = jnp.exp2(log_beta - right_of - jnp.concatenate([run] * (K_TILE // LANES), axis=1))
            acc_ref[h, rows, :] += _dot(a.astype(_bf16), values)
            run_ref[h, rows, :] = run + block_sum

    def run_min(rows):
        parts = [run_ref[h, rows, :] for h in heads]
        while len(parts) > 1:
            parts = [jnp.minimum(a, b) for a, b in zip(parts[0::2], parts[1::2])]
        m = parts[0]
        while m.shape[0] > SUBLANES:
            half = m.shape[0] // 2
            m = jnp.minimum(m[:half], m[half:])
        return jnp.min(m)

    def sweep(blocks, tile_start, value_skew, rows=every_row):
        work = [(i, h) for i in range(len(blocks)) for h in heads]
        z, staged = {}, {}
        for t in range(len(work) + value_skew):
            if 0 <= t - value_skew:
                i, h = work[t - value_skew]
                accumulate(blocks[i], h, *staged.pop((i, h)), first=(tile_start and i == 0), rows=rows)
            if t < len(work):
                i, h = work[t]
                z[i, h] = logits(blocks[i], h, own_tile=(tile_start and i == 0), rows=rows)
            if 0 <= t - HEAD_SKEW < len(work):
                i, h = work[t - HEAD_SKEW]
                staged[i, h] = suffix_sums(z.pop((i, h)), diagonal=(tile_start and i == 0))

    @pl.when(qi >= FUSED_BLOCKS - 1)
    def _():
        sweep([qi - i for i in range(FUSED_BLOCKS)], tile_start=True, value_skew=VALUE_SKEW)

    @pl.when(qi < FUSED_BLOCKS - 1)
    def _():
        sweep([qi], tile_start=True, value_skew=VALUE_SKEW)

    row_groups = [slice(r, r + tq // TAIL_ROW_GROUPS) for r in range(0, tq, tq // TAIL_ROW_GROUPS)]

    def run_mins():
        return tuple(run_min(rows) for rows in row_groups)

    def more(carry):
        kb, mins = carry
        return jnp.logical_and(kb >= 0, functools.reduce(jnp.minimum, mins) < UNDERFLOW_LOG2)

    def body(carry):
        kb, mins = carry
        for rows, group_min in zip(row_groups, mins):
            @pl.when(group_min < UNDERFLOW_LOG2)
            def _():
                sweep([kb], tile_start=False, value_skew=n_heads, rows=rows)
        return kb - 1, run_mins()

    first_left = jnp.where(qi >= FUSED_BLOCKS - 1, qi - FUSED_BLOCKS, qi - 1)
    lax.while_loop(more, body, (first_left, run_mins()))

    first_head = lax.broadcasted_iota(jnp.int32, (tq, LANES), 1) < HEAD_DIM
    for h in range(0, n_heads, HEADS_PER_TILE):
        o_ref[0, :, lanes_of(h)] = jnp.where(first_head, acc_ref[h], acc_ref[h + 1]).astype(o_ref.dtype)


def _attention(q, k, v):
    b, s, w = k.shape
    width = ATTN_LANE_TILES * LANES
    n_heads = ATTN_LANE_TILES * HEADS_PER_TILE
    assert HEADS_PER_TILE == 2 and Q_TILE == K_TILE and s % Q_TILE == 0 and w % width == 0
    tile = pl.BlockSpec((1, Q_TILE, width), lambda i, h, j: (i, j, h))
    return pl.pallas_call(
        _attn_kernel,
        grid=(b, w // width, s // Q_TILE),
        in_specs=[pl.BlockSpec((1, HEADS_PER_TILE, Q_TILE, width), lambda i, h, j: (i, 0, j, h)),
                  tile, tile],
        out_specs=tile,
        out_shape=jax.ShapeDtypeStruct((b, s, w), _bf16),
        scratch_shapes=[
            pltpu.VMEM((n_heads, Q_TILE, LANES), _f32),
            pltpu.VMEM((n_heads, Q_TILE, LANES), _f32),
            pltpu.VMEM((s, width), _bf16),
            pltpu.VMEM((s, width), _bf16),
        ],
        compiler_params=pltpu.CompilerParams(
            dimension_semantics=("arbitrary", "arbitrary", "arbitrary"),
            vmem_limit_bytes=VMEM_LIMIT),
        name="attention",
    )(q, k, v)


def _mixer_out_kernel(x_ref, o_ref, gc_ref, sgb_ref, mod_ref, gains_ref,
                      wpa_ref, wout_ref, w1_ref, w2_ref, out_ref):
    d = x_ref.shape[-1]
    mod = lambda i: mod_ref[0, :, i * d:(i + 1) * d]
    gate1, shift2, scale2, gate2 = mod(2), mod(3), mod(4), mod(5)
    g_post_mix = gains_ref[0:1, :]
    g_pre_mlp = gains_ref[1:2, :]
    g_post_mlp = gains_ref[2:3, :]

    tm = x_ref.shape[1]
    subs = [pl.ds(r, tm // ROW_SUBTILES) for r in range(0, tm, tm // ROW_SUBTILES)]

    mix = []
    for rows in subs:
        y_attn = _dot(o_ref[0, rows, :], wpa_ref[...])
        merged = gc_ref[0, rows, :] + sgb_ref[0, rows, :] * y_attn
        mix.append(_dot(merged, wout_ref[...]))
    x1, h2 = [], []
    for i, rows in enumerate(subs):
        x1.append(x_ref[0, rows, :] + gate1 * _rms(mix[i], g_post_mix))
        h2.append((_rms(x1[i], g_pre_mlp) * (1.0 + scale2) + shift2).astype(_bf16))
    for i, rows in enumerate(subs):
        ff = jnp.zeros_like(x1[i])
        for j in range(0, w1_ref.shape[1], FF_CHUNK):
            hid = jnp.maximum(_dot(h2[i], w1_ref[:, j:j + FF_CHUNK]), 0.0)
            ff = ff + _dot((hid * hid).astype(_bf16), w2_ref[j:j + FF_CHUNK, :])
        out_ref[0, rows, :] = x1[i] + gate2 * _rms(ff, g_post_mlp)


def _mixer_out(layer, x, o, gc, sgb, mod, gains, w_proj_attn, w_out, w1, w2):
    b, s, d = x.shape
    tm = SEQ_TILE
    assert s % tm == 0 and tm % (ROW_SUBTILES * SUBLANES) == 0 and w1.shape[-1] % FF_CHUNK == 0
    tile = lambda width: pl.BlockSpec((1, tm, width), lambda i, j: (i, j, 0))
    return pl.pallas_call(
        _mixer_out_kernel,
        grid=(b, s // tm),
        in_specs=[
            tile(d), tile(ATTN_WIDTH), tile(d), tile(d),
            pl.BlockSpec((1, 1, N_MOD * d), lambda i, j: (i, 0, 0)),
            _const_spec(gains.shape),
            _layer_spec(w_proj_attn, layer),
            _layer_spec(w_out, layer),
            _const_spec(w1.shape),
            _const_spec(w2.shape),
        ],
        out_specs=tile(d),
        out_shape=jax.ShapeDtypeStruct((b, s, d), _f32),
        compiler_params=pltpu.CompilerParams(
            dimension_semantics=("arbitrary", "arbitrary"), vmem_limit_bytes=VMEM_LIMIT),
        name="mixer_out",
    )(x, o, gc, sgb, mod.reshape(b, 1, N_MOD * d), gains, w_proj_attn, w_out, w1, w2)


def kernel(x, c, w_ada, b_ada, g_pre_mix, g_post_mix, g_pre_mlp, g_post_mlp, w_in, conv_w,
           w_proj_conv, w_proj_attn, w_out, w_mlp_in, w_mlp_out):
    depth = w_ada.shape[0]
    mod = _ada(c, w_ada, b_ada)
    for l in range(depth):
        q, k, v, gc, sgb, w1, w2 = _mixer_in(l, x, mod[l], g_pre_mix[l], w_in, conv_w, w_proj_conv,
                                             w_mlp_in, w_mlp_out)
        o = _attention(q, k, v)
        gains = jnp.stack([g_post_mix[l], g_pre_mlp[l], g_post_mlp[l]])
        x = _mixer_out(l, x, o, gc, sgb, mod[l], gains, w_proj_attn, w_out, w1, w2)
    return x
```
